```python
import jax, jax.numpy as jnp
from jax import lax
import numpy as np

D_MODEL = 1024
BATCH = 8
SEQ = 4096
DEPTH = 1

CHUNK = 64
Q_BLOCK = 128
LRU_WIDTH = 512
LRU_BLOCKS = 8
LRU_BLOCK_DIM = LRU_WIDTH // LRU_BLOCKS
CONV_WIDTH = 4
LRU_C = 8.0
MLA_HEADS = 8
QK_NOPE_DIM = 64
QK_ROPE_DIM = 32
V_HEAD_DIM = 64
Q_LORA_RANK = 384
KV_LORA_RANK = 256
ROPE_THETA = 10000.0
MLA_WIDTH = MLA_HEADS * V_HEAD_DIM
D_MIX = LRU_WIDTH + MLA_WIDTH
IN_COLS = 2 * LRU_WIDTH + Q_LORA_RANK + KV_LORA_RANK + QK_ROPE_DIM
D_FF = 2816
EPS = 1e-6

kernel_name = "hybrid_rglru_mla_macaron_sandwich"


def rms_norm(x, g):
    xf = x.astype(jnp.float32)
    y = xf * lax.rsqrt(jnp.mean(xf * xf, axis=-1, keepdims=True) + EPS)
    return (y * g.astype(jnp.float32)).astype(x.dtype)


def swiglu(x, w_gate, w_up, w_down):
    return (jax.nn.silu(x @ w_gate) * (x @ w_up)) @ w_down


def causal_depthwise_conv(x, w, b):
    s = x.shape[1]
    xp = jnp.pad(x, ((0, 0), (CONV_WIDTH - 1, 0), (0, 0)))
    y = b
    for k in range(CONV_WIDTH):
        y = y + xp[:, k:k + s, :] * w[k]
    return y


def rg_lru(x, w_a, b_a, w_x, b_x, lam):
    bsz, s, _ = x.shape
    xb = x.reshape(bsz, s, LRU_BLOCKS, LRU_BLOCK_DIM)
    r = jax.nn.sigmoid((jnp.einsum('bsnd,nde->bsne', xb, w_a).reshape(bsz, s, LRU_WIDTH) + b_a).astype(jnp.float32))
    i = jax.nn.sigmoid((jnp.einsum('bsnd,nde->bsne', xb, w_x).reshape(bsz, s, LRU_WIDTH) + b_x).astype(jnp.float32))
    log_a = -LRU_C * r * jax.nn.softplus(-lam.astype(jnp.float32))
    a = jnp.exp(log_a)
    mult = jnp.sqrt(-jnp.expm1(2.0 * log_a))
    u = mult * (i * x.astype(jnp.float32))

    def combine(left, right):
        a1, b1 = left
        a2, b2 = right
        return a1 * a2, a2 * b1 + b2

    _, h = lax.associative_scan(combine, (a, u), axis=1)
    return h.astype(x.dtype)


def rope(x, cos, sin):
    half = x.shape[-1] // 2
    x1, x2 = x[..., :half], x[..., half:]
    return jnp.concatenate([x1 * cos - x2 * sin, x2 * cos + x1 * sin], axis=-1).astype(x.dtype)


def mla(q_lat, kv_lat, k_rope_raw, positions, q_a_norm, w_q_b, kv_a_norm, w_kv_b):
    bsz, s, _ = q_lat.shape
    q = (rms_norm(q_lat, q_a_norm) @ w_q_b).reshape(bsz, s, MLA_HEADS, QK_NOPE_DIM + QK_ROPE_DIM)
    q_nope, q_pe = q[..., :QK_NOPE_DIM], q[..., QK_NOPE_DIM:]
    kv = (rms_norm(kv_lat, kv_a_norm) @ w_kv_b).reshape(bsz, s, MLA_HEADS, QK_NOPE_DIM + V_HEAD_DIM)
    k_nope, v = kv[..., :QK_NOPE_DIM], kv[..., QK_NOPE_DIM:]

    inv_freq = 1.0 / (ROPE_THETA ** (jnp.arange(0, QK_ROPE_DIM, 2, dtype=jnp.float32) / QK_ROPE_DIM))
    ang = positions.astype(jnp.float32)[..., None] * inv_freq
    cos, sin = jnp.cos(ang), jnp.sin(ang)
    q_pe = rope(q_pe, cos[:, :, None, :], sin[:, :, None, :])
    k_pe = rope(k_rope_raw, cos, sin)

    scale = (QK_NOPE_DIM + QK_ROPE_DIM) ** -0.5
    n_blocks = s // Q_BLOCK
    qn_blocks = q_nope.reshape(bsz, n_blocks, Q_BLOCK, MLA_HEADS, QK_NOPE_DIM).transpose(1, 0, 2, 3, 4)
    qp_blocks = q_pe.reshape(bsz, n_blocks, Q_BLOCK, MLA_HEADS, QK_ROPE_DIM).transpose(1, 0, 2, 3, 4)
    key_chunk = jnp.arange(s) // CHUNK

    def attend(args):
        qn, qp, blk = args
        sc = (jnp.einsum('bqhd,bkhd->bhqk', qn, k_nope)
              + jnp.einsum('bqhd,bkd->bhqk', qp, k_pe)).astype(jnp.float32) * scale
        q_chunk = (blk * Q_BLOCK + jnp.arange(Q_BLOCK)) // CHUNK
        mask = key_chunk[None, :] <= q_chunk[:, None]
        sc = jnp.where(mask[None, None], sc, -1e30)
        p = jax.nn.softmax(sc, axis=-1).astype(v.dtype)
        return jnp.einsum('bhqk,bkhd->bqhd', p, v)

    out = lax.map(attend, (qn_blocks, qp_blocks, jnp.arange(n_blocks)))
    return out.transpose(1, 0, 2, 3, 4).reshape(bsz, s, MLA_WIDTH)


def hybrid_mixer(h, positions, w_in, conv_w, conv_b, w_lru_a, b_lru_a, w_lru_x, b_lru_x,
                 lru_lambda, q_a_norm, w_q_b, kv_a_norm, w_kv_b, w_out):
    proj = h @ w_in
    cuts = [LRU_WIDTH, 2 * LRU_WIDTH, 2 * LRU_WIDTH + Q_LORA_RANK,
            2 * LRU_WIDTH + Q_LORA_RANK + KV_LORA_RANK]
    x_lru, gate, q_lat, kv_lat, k_rope_raw = jnp.split(proj, cuts, axis=-1)
    y_lru = rg_lru(causal_depthwise_conv(x_lru, conv_w, conv_b),
                   w_lru_a, b_lru_a, w_lru_x, b_lru_x, lru_lambda) * jax.nn.gelu(gate)
    y_mla = mla(q_lat, kv_lat, k_rope_raw, positions, q_a_norm, w_q_b, kv_a_norm, w_kv_b)
    return jnp.concatenate([y_lru, y_mla], axis=-1) @ w_out


def setup_inputs(seed: int = 0) -> dict:
    key = jax.random.key(seed)
    ks = iter(jax.random.split(key, 40))
    f32 = jnp.float32

    def nrm(shape, fan_in):
        return jax.random.normal(next(ks), shape, f32) * (fan_in ** -0.5)

    def gain(shape):
        return 1.0 + 0.01 * jax.random.normal(next(ks), shape, f32)

    def bias(shape):
        return 0.01 * jax.random.normal(next(ks), shape, f32)

    L = DEPTH
    x = jax.random.normal(next(ks), (BATCH, SEQ, D_MODEL), f32)
    offset = jax.random.randint(next(ks), (BATCH, 1), 0, 65536, dtype=jnp.int32)
    positions = (offset + jnp.arange(SEQ, dtype=jnp.int32)[None, :]).astype(jnp.int32)

    a0 = jax.random.uniform(next(ks), (L, LRU_WIDTH), f32, 0.9, 0.999)
    sa = a0 ** (1.0 / LRU_C)
    lru_lambda = jnp.log(sa) - jnp.log1p(-sa)

    return {
        "x": x,
        "positions": positions,
        "g_ffn1_pre": gain((L, D_MODEL)),
        "g_ffn1_post": gain((L, D_MODEL)),
        "w_ffn1_gate": nrm((L, D_MODEL, D_FF), D_MODEL),
        "w_ffn1_up": nrm((L, D_MODEL, D_FF), D_MODEL),
        "w_ffn1_down": nrm((L, D_FF, D_MODEL), D_FF),
        "g_mix_pre": gain((L, D_MODEL)),
        "g_mix_post": gain((L, D_MODEL)),
        "w_in": nrm((L, D_MODEL, IN_COLS), D_MODEL),
        "conv_w": nrm((L, CONV_WIDTH, LRU_WIDTH), CONV_WIDTH),
        "conv_b": bias((L, LRU_WIDTH)),
        "w_lru_a": nrm((L, LRU_BLOCKS, LRU_BLOCK_DIM, LRU_BLOCK_DIM), LRU_BLOCK_DIM),
        "b_lru_a": bias((L, LRU_WIDTH)),
        "w_lru_x": nrm((L, LRU_BLOCKS, LRU_BLOCK_DIM, LRU_BLOCK_DIM), LRU_BLOCK_DIM),
        "b_lru_x": bias((L, LRU_WIDTH)),
        "lru_lambda": lru_lambda,
        "q_a_norm": gain((L, Q_LORA_RANK)),
        "w_q_b": nrm((L, Q_LORA_RANK, MLA_HEADS * (QK_NOPE_DIM + QK_ROPE_DIM)), Q_LORA_RANK),
        "kv_a_norm": gain((L, KV_LORA_RANK)),
        "w_kv_b": nrm((L, KV_LORA_RANK, MLA_HEADS * (QK_NOPE_DIM + V_HEAD_DIM)), KV_LORA_RANK),
        "w_out": nrm((L, D_MIX, D_MODEL), D_MIX),
        "g_ffn2_pre": gain((L, D_MODEL)),
        "g_ffn2_post": gain((L, D_MODEL)),
        "w_ffn2_gate": nrm((L, D_MODEL, D_FF), D_MODEL),
        "w_ffn2_up": nrm((L, D_MODEL, D_FF), D_MODEL),
        "w_ffn2_down": nrm((L, D_FF, D_MODEL), D_FF),
    }


def reference(x, positions, g_ffn1_pre, g_ffn1_post, w_ffn1_gate, w_ffn1_up, w_ffn1_down,
              g_mix_pre, g_mix_post, w_in, conv_w, conv_b, w_lru_a, b_lru_a, w_lru_x, b_lru_x,
              lru_lambda, q_a_norm, w_q_b, kv_a_norm, w_kv_b, w_out,
              g_ffn2_pre, g_ffn2_post, w_ffn2_gate, w_ffn2_up, w_ffn2_down):
    h = x
    for l in range(DEPTH):
        f = swiglu(rms_norm(h, g_ffn1_pre[l]), w_ffn1_gate[l], w_ffn1_up[l], w_ffn1_down[l])
        h = h + 0.5 * rms_norm(f, g_ffn1_post[l])
        m = hybrid_mixer(rms_norm(h, g_mix_pre[l]), positions, w_in[l], conv_w[l], conv_b[l],
                         w_lru_a[l], b_lru_a[l], w_lru_x[l], b_lru_x[l], lru_lambda[l],
                         q_a_norm[l], w_q_b[l], kv_a_norm[l], w_kv_b[l], w_out[l])
        h = h + rms_norm(m, g_mix_post[l])
        f = swiglu(rms_norm(h, g_ffn2_pre[l]), w_ffn2_gate[l], w_ffn2_up[l], w_ffn2_down[l])
        h = h + 0.5 * rms_norm(f, g_ffn2_post[l])
    return h
```

```python
import functools

import jax
import jax.numpy as jnp
import numpy as np
from jax import lax
from jax.experimental import pallas as pl
from jax.experimental.pallas import tpu as pltpu

F32 = jnp.float32
BF16 = jnp.bfloat16

EPS = 1e-6
CHUNK = 64
LRU_C = 8.0
ROPE_THETA = 10000.0
MLA_HEADS = 8
QK_NOPE_DIM = 64
QK_ROPE_DIM = 32
V_HEAD_DIM = 64
HEAD_LANES = 128
MASK_VALUE = -1e30

VMEM_LIMIT_BYTES = 56 * 1024 * 1024
TOKEN_TILE = 512
ATTN_TILE = 512
SUBLANES = 8


def _rms(x, g):
    return x * lax.rsqrt(jnp.mean(x * x, axis=-1, keepdims=True) + EPS) * g


def _const_spec(shape):
    nd = len(shape)
    return pl.BlockSpec(shape, lambda *_: (0,) * nd, pipeline_mode=pl.Buffered(1))


def _rope_table_kernel(pos_ref, invf_ref, cos_ref, sin_ref):
    ang = pos_ref[...].astype(F32) * invf_ref[...]
    cos_ref[...] = jnp.cos(ang)
    sin_ref[...] = jnp.sin(ang)


def _rope_tables(positions):
    half = QK_ROPE_DIM // 2
    t = positions.size
    per_row = 128 // half
    inv_freq = 1.0 / (ROPE_THETA ** (jnp.arange(0, QK_ROPE_DIM, 2, dtype=F32) / QK_ROPE_DIM))
    pos_rep = jnp.repeat(positions.reshape(t, 1), half, axis=1).reshape(t // per_row, 128)
    invf = jnp.tile(inv_freq, per_row).reshape(1, 128)
    rows = t // per_row
    tr = min(rows, 512)
    cos_d, sin_d = pl.pallas_call(
        _rope_table_kernel,
        grid=(rows // tr,),
        in_specs=[pl.BlockSpec((tr, 128), lambda i: (i, 0)), pl.BlockSpec((1, 128), lambda i: (0, 0))],
        out_specs=[pl.BlockSpec((tr, 128), lambda i: (i, 0))] * 2,
        out_shape=[jax.ShapeDtypeStruct((rows, 128), F32)] * 2,
        name="rope_tables",
    )(pos_rep, invf)
    cos16 = cos_d.reshape(t, half)
    sin16 = sin_d.reshape(t, half)
    pad = HEAD_LANES - QK_NOPE_DIM - QK_ROPE_DIM
    cos_t = jnp.concatenate([jnp.ones((t, QK_NOPE_DIM), F32), cos16, cos16, jnp.zeros((t, pad), F32)], axis=1)
    sin_t = jnp.concatenate([jnp.zeros((t, QK_NOPE_DIM), F32), sin16, sin16, jnp.zeros((t, pad), F32)], axis=1)
    return cos_t, sin_t


def _ffn_kernel(*refs, with_mix_out, ff_chunks):
    if with_mix_out:
        (h_ref, ylru_ref, ymla_ref, wout_ref, gmix_ref,
         gpre_ref, wg_ref, wu_ref, wd_ref, gpost_ref, o_ref) = refs
        lru_w = ylru_ref.shape[1]
        m = jnp.dot(ylru_ref[...], wout_ref[:lru_w, :], preferred_element_type=F32)
        m = m + jnp.dot(ymla_ref[...], wout_ref[lru_w:, :], preferred_element_type=F32)
        h = h_ref[...] + _rms(m, gmix_ref[...])
    else:
        h_ref, gpre_ref, wg_ref, wu_ref, wd_ref, gpost_ref, o_ref = refs
        h = h_ref[...]
    n = _rms(h, gpre_ref[...]).astype(BF16)
    f = None
    for lo, hi in ff_chunks:
        g = jnp.dot(n, wg_ref[:, lo:hi], preferred_element_type=F32)
        u = jnp.dot(n, wu_ref[:, lo:hi], preferred_element_type=F32)
        a = (g * jax.nn.sigmoid(g) * u).astype(BF16)
        part = jnp.dot(a, wd_ref[lo:hi, :], preferred_element_type=F32)
        f = part if f is None else f + part
    o_ref[...] = h + 0.5 * _rms(f, gpost_ref[...])


def _ffn_chunks(d_ff, target=1536):
    chunks, lo = [], 0
    while lo < d_ff:
        hi = min(lo + target, d_ff)
        chunks.append((lo, hi))
        lo = hi
    return tuple(chunks)


def _ffn(h, gpre, wg, wu, wd, gpost, mix=None):
    t, d = h.shape
    d_ff = wg.shape[1]
    tm = min(TOKEN_TILE, t)
    tok = lambda w: pl.BlockSpec((tm, w), lambda i: (i, 0))
    ins, specs = [h], [tok(d)]
    if mix is not None:
        ylru, ymla, wout, gmix = mix
        ins += [ylru, ymla, wout, gmix]
        specs += [tok(ylru.shape[1]), tok(ymla.shape[1]), _const_spec(wout.shape), _const_spec(gmix.shape)]
    ins += [gpre, wg, wu, wd, gpost]
    specs += [_const_spec(a.shape) for a in (gpre, wg, wu, wd, gpost)]
    return pl.pallas_call(
        functools.partial(_ffn_kernel, with_mix_out=mix is not None, ff_chunks=_ffn_chunks(d_ff)),
        grid=(t // tm,),
        in_specs=specs,
        out_specs=tok(d),
        out_shape=jax.ShapeDtypeStruct((t, d), F32),
        compiler_params=pltpu.CompilerParams(
            dimension_semantics=("arbitrary",), vmem_limit_bytes=VMEM_LIMIT_BYTES),
        name="ffn_mix_out" if mix is not None else "ffn",
    )(*ins)


def _mixer_in_kernel(h_ref, cos_ref, sin_ref, gpre_ref, win_ref, convw_ref, convb_ref, wgate_ref,
                     ba_ref, bx_ref, lam_ref, qg_ref, wqa_ref, wqb_ref, kvg_ref, wk_ref, wv_ref,
                     q_out, k_out, v_out, ylru_out, xbuf, abuf, ubuf, hcar, *, q_scale):
    tm = h_ref.shape[0]
    lru_w = ylru_out.shape[1]
    q_rank = qg_ref.shape[1]
    kv_rank = kvg_ref.shape[1]
    heads = q_out.shape[1] // HEAD_LANES

    @pl.when(pl.program_id(1) == 0)
    def _():
        xbuf[0:SUBLANES, :] = jnp.zeros((SUBLANES, lru_w), F32)
        hcar[...] = jnp.zeros_like(hcar)

    n = _rms(h_ref[...], gpre_ref[...]).astype(BF16)
    proj = jnp.dot(n, win_ref[...], preferred_element_type=F32)
    c0 = 2 * lru_w
    c1 = c0 + q_rank
    c2 = c1 + kv_rank
    xl = proj[:, :lru_w]
    gate = proj[:, lru_w:c0]
    kpe = proj[:, c2:c2 + HEAD_LANES]
    kpe_rot = proj[:, c2 + HEAD_LANES:c2 + 2 * HEAD_LANES]
    cos_t = cos_ref[...]
    sin_t = sin_ref[...]

    qn = _rms(proj[:, c0:c1], qg_ref[...]).astype(BF16)
    qa = jnp.dot(qn, wqa_ref[...], preferred_element_type=F32)
    qb = jnp.dot(qn, wqb_ref[...], preferred_element_type=F32)
    cos_h = jnp.tile(cos_t, (1, heads))
    sin_h = jnp.tile(sin_t, (1, heads))
    q_out[...] = ((qa * cos_h + qb * sin_h) * q_scale).astype(BF16)

    kvn = _rms(proj[:, c1:c2], kvg_ref[...]).astype(BF16)
    k_rope = kpe * cos_t + kpe_rot * sin_t
    k_nope = jnp.dot(kvn, wk_ref[...], preferred_element_type=F32)
    k_out[...] = (k_nope + jnp.tile(k_rope, (1, heads))).astype(BF16)
    v_out[...] = jnp.dot(kvn, wv_ref[...], preferred_element_type=F32).astype(BF16)

    xbuf[SUBLANES:SUBLANES + tm, :] = xl
    cw = convw_ref[...]
    taps = cw.shape[0]
    xc = convb_ref[...]
    for k in range(taps):
        off = SUBLANES - (taps - 1) + k
        xc = xc + xbuf[off:off + tm, :] * cw[k:k + 1, :]
    xbuf[0:SUBLANES, :] = xbuf[tm:tm + SUBLANES, :]

    xcb = xc.astype(BF16)
    half = lru_w // 2
    g0 = jnp.dot(xcb[:, :half], wgate_ref[0], preferred_element_type=F32)
    g1 = jnp.dot(xcb[:, half:], wgate_ref[1], preferred_element_type=F32)
    r = jax.nn.sigmoid(jnp.concatenate([g0[:, :half], g1[:, :half]], axis=1) + ba_ref[...])
    i = jax.nn.sigmoid(jnp.concatenate([g0[:, half:], g1[:, half:]], axis=1) + bx_ref[...])
    log_a = (-LRU_C * r) * jax.nn.softplus(-lam_ref[...])
    a = jnp.exp(log_a)
    th = jnp.tanh(log_a)
    u = jnp.sqrt(-2.0 * th / (1.0 - th)) * (i * xc)

    groups = tm // SUBLANES
    a3 = a.reshape(groups, SUBLANES, lru_w)
    u3 = u.reshape(groups, SUBLANES, lru_w)
    row = lax.broadcasted_iota(jnp.int32, a3.shape, 1)
    s = 1
    while s < SUBLANES:
        keep = row >= s
        u3 = jnp.where(keep, a3 * pltpu.roll(u3, s, 1) + u3, u3)
        a3 = jnp.where(keep, a3 * pltpu.roll(a3, s, 1), a3)
        s *= 2
    abuf[...] = a3.reshape(tm, lru_w)
    ubuf[...] = u3.reshape(tm, lru_w)

    def carry_group(j, carry):
        off = pl.multiple_of(j * SUBLANES, SUBLANES)
        hj = ubuf[pl.ds(off, SUBLANES), :] + abuf[pl.ds(off, SUBLANES), :] * carry
        ubuf[pl.ds(off, SUBLANES), :] = hj
        return jnp.broadcast_to(hj[SUBLANES - 1:SUBLANES, :], hj.shape)

    hcar[...] = lax.fori_loop(0, groups, carry_group, hcar[...], unroll=8)
    ylru_out[...] = (ubuf[...] * jax.nn.gelu(gate)).astype(BF16)


def _mixer_in(h, cos_t, sin_t, batch, gpre, win, convw, convb, wgate, ba, bx, lam, qg, wqa, wqb, kvg, wk, wv):
    t, d = h.shape
    seq = t // batch
    tm = min(TOKEN_TILE, seq)
    ns = seq // tm
    lru_w = convw.shape[1]
    tok = lambda w: pl.BlockSpec((tm, w), lambda b, s: (b * ns + s, 0))
    consts = (gpre, win, convw, convb, wgate, ba, bx, lam, qg, wqa, wqb, kvg, wk, wv)
    q_scale = float((QK_NOPE_DIM + QK_ROPE_DIM) ** -0.5)
    return pl.pallas_call(
        functools.partial(_mixer_in_kernel, q_scale=q_scale),
        grid=(batch, ns),
        in_specs=[tok(d), tok(HEAD_LANES), tok(HEAD_LANES)] + [_const_spec(a.shape) for a in consts],
        out_specs=[tok(wqa.shape[1]), tok(wk.shape[1]), tok(wv.shape[1]), tok(lru_w)],
        out_shape=[jax.ShapeDtypeStruct((t, wqa.shape[1]), BF16),
                   jax.ShapeDtypeStruct((t, wk.shape[1]), BF16),
                   jax.ShapeDtypeStruct((t, wv.shape[1]), BF16),
                   jax.ShapeDtypeStruct((t, lru_w), BF16)],
        scratch_shapes=[pltpu.VMEM((tm + SUBLANES, lru_w), F32),
                        pltpu.VMEM((tm, lru_w), F32),
                        pltpu.VMEM((tm, lru_w), F32),
                        pltpu.VMEM((SUBLANES, lru_w), F32)],
        compiler_params=pltpu.CompilerParams(
            dimension_semantics=("arbitrary", "arbitrary"), vmem_limit_bytes=VMEM_LIMIT_BYTES),
        name="mixer_in",
    )(h, cos_t, sin_t, *consts)


def _attn_kernel(qi_ref, kj_ref, q_ref, k_ref, v_ref, o_ref, m_ref, l_ref, acc_ref):
    tq = q_ref.shape[0]
    tk = k_ref.shape[0]
    heads = q_ref.shape[1] // HEAD_LANES
    p_idx = pl.program_id(1)
    qi = qi_ref[p_idx]
    kj = kj_ref[p_idx]

    @pl.when(kj == 0)
    def _():
        m_ref[...] = jnp.full(m_ref.shape, MASK_VALUE, F32)
        l_ref[...] = jnp.zeros_like(l_ref)
        acc_ref[...] = jnp.zeros_like(acc_ref)

    def step(on_diagonal):
        if on_diagonal:
            rq = lax.broadcasted_iota(jnp.int32, (tq, tk), 0) // CHUNK
            ck = lax.broadcasted_iota(jnp.int32, (tq, tk), 1) // CHUNK
            visible = ck <= rq
        for h in range(heads):
            qh = q_ref[:, h * HEAD_LANES:(h + 1) * HEAD_LANES]
            kh = k_ref[:, h * HEAD_LANES:(h + 1) * HEAD_LANES]
            s = lax.dot_general(qh, kh, (((1,), (1,)), ((), ())), preferred_element_type=F32)
            if on_diagonal:
                s = jnp.where(visible, s, MASK_VALUE)
            m_prev = m_ref[h]
            m_new = jnp.maximum(m_prev, jnp.max(s, axis=1, keepdims=True))
            alpha = jnp.exp(m_prev - m_new)
            p = jnp.exp(s - m_new[:, :1])
            l_ref[h] = alpha * l_ref[h] + jnp.sum(p, axis=1, keepdims=True)
            m_ref[h] = m_new
            vh = v_ref[:, h * V_HEAD_DIM:(h + 1) * V_HEAD_DIM]
            pv = jnp.dot(p.astype(BF16), vh, preferred_element_type=F32)
            sl = slice(h * V_HEAD_DIM, (h + 1) * V_HEAD_DIM)
            acc_ref[:, sl] = acc_ref[:, sl] * alpha[:, :V_HEAD_DIM] + pv

    @pl.when(kj < qi)
    def _():
        step(False)

    @pl.when(kj == qi)
    def _():
        step(True)
        for h in range(heads):
            sl = slice(h * V_HEAD_DIM, (h + 1) * V_HEAD_DIM)
            o_ref[:, sl] = (acc_ref[:, sl] / l_ref[h][:, :V_HEAD_DIM]).astype(o_ref.dtype)


def _attention(q, k, v, batch):
    t = q.shape[0]
    seq = t // batch
    tile = min(ATTN_TILE, seq)
    assert tile % CHUNK == 0
    nt = seq // tile
    heads = q.shape[1] // HEAD_LANES
    pairs = [(i, j) for i in range(nt) for j in range(i + 1)]
    qi = jnp.asarray(np.array([p[0] for p in pairs], np.int32))
    kj = jnp.asarray(np.array([p[1] for p in pairs], np.int32))
    q_map = lambda b, p, qi_r, kj_r: (b * nt + qi_r[p], 0)
    k_map = lambda b, p, qi_r, kj_r: (b * nt + kj_r[p], 0)
    return pl.pallas_call(
        _attn_kernel,
        grid_spec=pltpu.PrefetchScalarGridSpec(
            num_scalar_prefetch=2,
            grid=(batch, len(pairs)),
            in_specs=[pl.BlockSpec((tile, q.shape[1]), q_map),
                      pl.BlockSpec((tile, k.shape[1]), k_map),
                      pl.BlockSpec((tile, v.shape[1]), k_map)],
            out_specs=pl.BlockSpec((tile, v.shape[1]), q_map),
            scratch_shapes=[pltpu.VMEM((heads, tile, 128), F32),
                            pltpu.VMEM((heads, tile, 128), F32),
                            pltpu.VMEM((tile, v.shape[1]), F32)]),
        out_shape=jax.ShapeDtypeStruct((t, v.shape[1]), BF16),
        compiler_params=pltpu.CompilerParams(
            dimension_semantics=("arbitrary", "arbitrary"), vmem_limit_bytes=VMEM_LIMIT_BYTES),
        name="attention",
    )(qi, kj, q, k, v)


def _rot_half_cols(w):
    half = w.shape[-1] // 2
    return jnp.concatenate([-w[..., half:], w[..., :half]], axis=-1)


def _head_slab(nope, rope):
    rows, heads = rope.shape[0], rope.shape[1]
    if nope is None:
        nope = jnp.zeros((rows, heads, QK_NOPE_DIM), rope.dtype)
    pad = jnp.zeros((rows, heads, HEAD_LANES - QK_NOPE_DIM - QK_ROPE_DIM), rope.dtype)
    return jnp.concatenate([nope, rope, pad], axis=-1).reshape(rows, heads * HEAD_LANES)


def _block_diag_pairs(w_a, w_x):
    blocks, bd, _ = w_a.shape
    eye = jnp.eye(blocks, dtype=w_a.dtype)
    width = blocks * bd
    half = width // 2
    dense = lambda w: jnp.einsum('nde,nm->ndme', w, eye).reshape(width, width)
    da, dx = dense(w_a), dense(w_x)
    tiles = [jnp.concatenate([da[lo:lo + half, lo:lo + half], dx[lo:lo + half, lo:lo + half]], axis=1)
             for lo in (0, half)]
    return jnp.stack(tiles)


def kernel(x, positions, g_ffn1_pre, g_ffn1_post, w_ffn1_gate, w_ffn1_up, w_ffn1_down, g_mix_pre, g_mix_post, w_in, conv_w, conv_b, w_lru_a, b_lru_a, w_lru_x, b_lru_x, lru_lambda, q_a_norm, w_q_b, kv_a_norm, w_kv_b, w_out, g_ffn2_pre, g_ffn2_post, w_ffn2_gate, w_ffn2_up, w_ffn2_down):
    batch, seq, d = x.shape
    depth = g_ffn1_pre.shape[0]
    lru_w = conv_w.shape[-1]
    q_rank = q_a_norm.shape[-1]
    kv_rank = kv_a_norm.shape[-1]
    row = lambda v: v.reshape(1, -1)

    cos_t, sin_t = _rope_tables(positions)
    h = x.reshape(batch * seq, d)
    for l in range(depth):
        h = _ffn(h, row(g_ffn1_pre[l]), w_ffn1_gate[l].astype(BF16), w_ffn1_up[l].astype(BF16),
                 w_ffn1_down[l].astype(BF16), row(g_ffn1_post[l]))

        wi = w_in[l]
        c_rope = 2 * lru_w + q_rank + kv_rank
        k_rope_w = wi[:, c_rope:].reshape(d, 1, QK_ROPE_DIM)
        win = jnp.concatenate([wi[:, :c_rope], _head_slab(None, k_rope_w),
                               _head_slab(None, _rot_half_cols(k_rope_w))], axis=1).astype(BF16)
        wq = w_q_b[l].reshape(q_rank, MLA_HEADS, QK_NOPE_DIM + QK_ROPE_DIM)
        wqa = _head_slab(wq[..., :QK_NOPE_DIM], wq[..., QK_NOPE_DIM:]).astype(BF16)
        wqb = _head_slab(None, _rot_half_cols(wq[..., QK_NOPE_DIM:])).astype(BF16)
        wkv = w_kv_b[l].reshape(kv_rank, MLA_HEADS, QK_NOPE_DIM + V_HEAD_DIM)
        wk = _head_slab(wkv[..., :QK_NOPE_DIM], jnp.zeros((kv_rank, MLA_HEADS, QK_ROPE_DIM), F32)).astype(BF16)
        wv = wkv[..., QK_NOPE_DIM:].reshape(kv_rank, MLA_HEADS * V_HEAD_DIM).astype(BF16)
        wgate = _block_diag_pairs(w_lru_a[l], w_lru_x[l]).astype(BF16)

        q, k, v, y_lru = _mixer_in(h, cos_t, sin_t, batch, row(g_mix_pre[l]), win, conv_w[l], row(conv_b[l]),
                                   wgate, row(b_lru_a[l]), row(b_lru_x[l]), row(lru_lambda[l]),
                                   row(q_a_norm[l]), wqa, wqb, row(kv_a_norm[l]), wk, wv)
        y_mla = _attention(q, k, v, batch)
        h = _ffn(h, row(g_ffn2_pre[l]), w_ffn2_gate[l].astype(BF16), w_ffn2_up[l].astype(BF16),
                 w_ffn2_down[l].astype(BF16), row(g_ffn2_post[l]),
                 mix=(y_lru, y_mla, w_out[l].astype(BF16), row(g_mix_post[l])))
    return h.reshape(batch, seq, d)
```

```python
import functools

import jax
import jax.numpy as jnp
import numpy as np
from jax import lax
from jax.experimental import pallas as pl
from jax.experimental.pallas import tpu as pltpu

F32 = jnp.float32
BF16 = jnp.bfloat16

EPS = 1e-6
CHUNK = 64
LRU_C = 8.0
ROPE_THETA = 10000.0
MLA_HEADS = 8
QK_NOPE_DIM = 64
QK_ROPE_DIM = 32
V_HEAD_DIM = 64
HEAD_LANES = 128
MASK_VALUE = -1e30

VMEM_LIMIT_BYTES = 56 * 1024 * 1024
TOKEN_TILE = 512
ATTN_Q_TILE = 512
ATTN_K_TILE = 512
SUBLANES = 8
ONES_ROWS = 16
LOG2_E = 1.4426950408889634


def _rms(x, g):
    return x * lax.rsqrt(jnp.mean(x * x, axis=-1, keepdims=True) + EPS) * g


def _const_spec(shape):
    nd = len(shape)
    return pl.BlockSpec(shape, lambda *_: (0,) * nd, pipeline_mode=pl.Buffered(1))


def _rope_table_kernel(pos_ref, invf_ref, cos_ref, sin_ref):
    ang = pos_ref[...].astype(F32) * invf_ref[...]
    cos_ref[...] = jnp.cos(ang)
    sin_ref[...] = jnp.sin(ang)


def _rope_tables(positions):
    half = QK_ROPE_DIM // 2
    t = positions.size
    per_row = 128 // half
    inv_freq = 1.0 / (ROPE_THETA ** (jnp.arange(0, QK_ROPE_DIM, 2, dtype=F32) / QK_ROPE_DIM))
    pos_rep = jnp.repeat(positions.reshape(t, 1), half, axis=1).reshape(t // per_row, 128)
    invf = jnp.tile(inv_freq, per_row).reshape(1, 128)
    rows = t // per_row
    tr = min(rows, 512)
    cos_d, sin_d = pl.pallas_call(
        _rope_table_kernel,
        grid=(rows // tr,),
        in_specs=[pl.BlockSpec((tr, 128), lambda i: (i, 0)), pl.BlockSpec((1, 128), lambda i: (0, 0))],
        out_specs=[pl.BlockSpec((tr, 128), lambda i: (i, 0))] * 2,
        out_shape=[jax.ShapeDtypeStruct((rows, 128), F32)] * 2,
        name="rope_tables",
    )(pos_rep, invf)
    cos16 = cos_d.reshape(t, half)
    sin16 = sin_d.reshape(t, half)
    pad = HEAD_LANES - QK_NOPE_DIM - QK_ROPE_DIM
    cos_t = jnp.concatenate([jnp.ones((t, QK_NOPE_DIM), F32), cos16, cos16, jnp.zeros((t, pad), F32)], axis=1)
    sin_t = jnp.concatenate([jnp.zeros((t, QK_NOPE_DIM), F32), sin16, sin16, jnp.zeros((t, pad), F32)], axis=1)
    return cos_t, sin_t


def _ffn_kernel(*refs, with_mix_out, ff_chunks):
    if with_mix_out:
        (h_ref, ylru_ref, ymla_ref, wout_ref, gmix_ref,
         gpre_ref, wg_ref, wu_ref, wd_ref, gpost_ref, o_ref) = refs
        lru_w = ylru_ref.shape[1]
        m = jnp.dot(ylru_ref[...], wout_ref[:lru_w, :], preferred_element_type=F32)
        m = m + jnp.dot(ymla_ref[...], wout_ref[lru_w:, :], preferred_element_type=F32)
        h = h_ref[...] + _rms(m, gmix_ref[...])
    else:
        h_ref, gpre_ref, wg_ref, wu_ref, wd_ref, gpost_ref, o_ref = refs
        h = h_ref[...]
    n = _rms(h, gpre_ref[...]).astype(BF16)
    f = None
    for lo, hi in ff_chunks:
        g = jnp.dot(n, wg_ref[:, lo:hi], preferred_element_type=F32)
        u = jnp.dot(n, wu_ref[:, lo:hi], preferred_element_type=F32)
        a = (g * jax.nn.sigmoid(g) * u).astype(BF16)
        part = jnp.dot(a, wd_ref[lo:hi, :], preferred_element_type=F32)
        f = part if f is None else f + part
    o_ref[...] = h + 0.5 * _rms(f, gpost_ref[...])


def _ffn_chunks(d_ff, target=1536):
    chunks, lo = [], 0
    while lo < d_ff:
        hi = min(lo + target, d_ff)
        chunks.append((lo, hi))
        lo = hi
    return tuple(chunks)


def _ffn(h, gpre, wg, wu, wd, gpost, mix=None):
    t, d = h.shape
    d_ff = wg.shape[1]
    tm = min(TOKEN_TILE, t)
    tok = lambda w: pl.BlockSpec((tm, w), lambda i: (i, 0))
    ins, specs = [h], [tok(d)]
    if mix is not None:
        ylru, ymla, wout, gmix = mix
        ins += [ylru, ymla, wout, gmix]
        specs += [tok(ylru.shape[1]), tok(ymla.shape[1]), _const_spec(wout.shape), _const_spec(gmix.shape)]
    ins += [gpre, wg, wu, wd, gpost]
    specs += [_const_spec(a.shape) for a in (gpre, wg, wu, wd, gpost)]
    return pl.pallas_call(
        functools.partial(_ffn_kernel, with_mix_out=mix is not None, ff_chunks=_ffn_chunks(d_ff)),
        grid=(t // tm,),
        in_specs=specs,
        out_specs=tok(d),
        out_shape=jax.ShapeDtypeStruct((t, d), F32),
        compiler_params=pltpu.CompilerParams(
            dimension_semantics=("arbitrary",), vmem_limit_bytes=VMEM_LIMIT_BYTES),
        name="ffn_mix_out" if mix is not None else "ffn",
    )(*ins)


def _mixer_in_kernel(h_ref, cos_ref, sin_ref, gpre_ref, win_ref, convw_ref, convb_ref, wgate_ref,
                     ba_ref, bx_ref, lam_ref, qg_ref, wqa_ref, wqb_ref, kvg_ref, wk_ref, wv_ref,
                     q_out, k_out, v_out, ylru_out, xbuf, abuf, ubuf, hcar, *, q_scale):
    tm = h_ref.shape[0]
    lru_w = ylru_out.shape[1]
    q_rank = qg_ref.shape[1]
    kv_rank = kvg_ref.shape[1]
    heads = q_out.shape[1] // HEAD_LANES

    @pl.when(pl.program_id(1) == 0)
    def _():
        xbuf[0:SUBLANES, :] = jnp.zeros((SUBLANES, lru_w), F32)
        hcar[...] = jnp.zeros_like(hcar)

    n = _rms(h_ref[...], gpre_ref[...]).astype(BF16)
    proj = jnp.dot(n, win_ref[...], preferred_element_type=F32)
    c0 = 2 * lru_w
    c1 = c0 + q_rank
    c2 = c1 + kv_rank
    xl = proj[:, :lru_w]
    gate = proj[:, lru_w:c0]
    kpe = proj[:, c2:c2 + HEAD_LANES]
    kpe_rot = proj[:, c2 + HEAD_LANES:c2 + 2 * HEAD_LANES]
    cos_t = cos_ref[...]
    sin_t = sin_ref[...]

    qn = _rms(proj[:, c0:c1], qg_ref[...]).astype(BF16)
    qa = jnp.dot(qn, wqa_ref[...], preferred_element_type=F32)
    qb = jnp.dot(qn, wqb_ref[...], preferred_element_type=F32)
    cos_h = jnp.tile(cos_t, (1, heads))
    sin_h = jnp.tile(sin_t, (1, heads))
    q_out[...] = ((qa * cos_h + qb * sin_h) * q_scale).astype(BF16)

    kvn = _rms(proj[:, c1:c2], kvg_ref[...]).astype(BF16)
    k_rope = kpe * cos_t + kpe_rot * sin_t
    k_nope = jnp.dot(kvn, wk_ref[...], preferred_element_type=F32)
    k_out[...] = (k_nope + jnp.tile(k_rope, (1, heads))).astype(BF16)
    v_out[...] = lax.dot_general(wv_ref[...], kvn, (((1,), (1,)), ((), ())),
                                 preferred_element_type=F32).astype(BF16)

    xbuf[SUBLANES:SUBLANES + tm, :] = xl
    cw = convw_ref[...]
    taps = cw.shape[0]
    xc = convb_ref[...]
    for k in range(taps):
        off = SUBLANES - (taps - 1) + k
        xc = xc + xbuf[off:off + tm, :] * cw[k:k + 1, :]
    xbuf[0:SUBLANES, :] = xbuf[tm:tm + SUBLANES, :]

    xcb = xc.astype(BF16)
    half = lru_w // 2
    g0 = jnp.dot(xcb[:, :half], wgate_ref[0], preferred_element_type=F32)
    g1 = jnp.dot(xcb[:, half:], wgate_ref[1], preferred_element_type=F32)
    r = jax.nn.sigmoid(jnp.concatenate([g0[:, :half], g1[:, :half]], axis=1) + ba_ref[...])
    i = jax.nn.sigmoid(jnp.concatenate([g0[:, half:], g1[:, half:]], axis=1) + bx_ref[...])
    log_a = (-LRU_C * r) * jax.nn.softplus(-lam_ref[...])
    a = jnp.exp(log_a)
    th = jnp.tanh(log_a)
    u = jnp.sqrt(-2.0 * th / (1.0 - th)) * (i * xc)

    groups = tm // SUBLANES
    a3 = a.reshape(groups, SUBLANES, lru_w)
    u3 = u.reshape(groups, SUBLANES, lru_w)
    row = lax.broadcasted_iota(jnp.int32, a3.shape, 1)
    s = 1
    while s < SUBLANES:
        keep = row >= s
        u3 = jnp.where(keep, a3 * pltpu.roll(u3, s, 1) + u3, u3)
        a3 = jnp.where(keep, a3 * pltpu.roll(a3, s, 1), a3)
        s *= 2
    abuf[...] = a3.reshape(tm, lru_w)
    ubuf[...] = u3.reshape(tm, lru_w)

    def carry_group(j, carry):
        off = pl.multiple_of(j * SUBLANES, SUBLANES)
        hj = ubuf[pl.ds(off, SUBLANES), :] + abuf[pl.ds(off, SUBLANES), :] * carry
        ubuf[pl.ds(off, SUBLANES), :] = hj
        return jnp.broadcast_to(hj[SUBLANES - 1:SUBLANES, :], hj.shape)

    hcar[...] = lax.fori_loop(0, groups, carry_group, hcar[...], unroll=8)
    ylru_out[...] = (ubuf[...] * jax.nn.gelu(gate)).astype(BF16)


def _mixer_in(h, cos_t, sin_t, batch, gpre, win, convw, convb, wgate, ba, bx, lam, qg, wqa, wqb, kvg, wk, wv):
    t, d = h.shape
    seq = t // batch
    tm = min(TOKEN_TILE, seq)
    ns = seq // tm
    lru_w = convw.shape[1]
    tok = lambda w: pl.BlockSpec((tm, w), lambda b, s: (b * ns + s, 0))
    consts = (gpre, win, convw, convb, wgate, ba, bx, lam, qg, wqa, wqb, kvg, wk, wv)
    q_scale = float((QK_NOPE_DIM + QK_ROPE_DIM) ** -0.5 * LOG2_E)
    return pl.pallas_call(
        functools.partial(_mixer_in_kernel, q_scale=q_scale),
        grid=(batch, ns),
        in_specs=[tok(d), tok(HEAD_LANES), tok(HEAD_LANES)] + [_const_spec(a.shape) for a in consts],
        out_specs=[tok(wqa.shape[1]), tok(wk.shape[1]),
                   pl.BlockSpec((wv.shape[0], tm), lambda b, s: (0, b * ns + s)), tok(lru_w)],
        out_shape=[jax.ShapeDtypeStruct((t, wqa.shape[1]), BF16),
                   jax.ShapeDtypeStruct((t, wk.shape[1]), BF16),
                   jax.ShapeDtypeStruct((wv.shape[0], t), BF16),
                   jax.ShapeDtypeStruct((t, lru_w), BF16)],
        scratch_shapes=[pltpu.VMEM((tm + SUBLANES, lru_w), F32),
                        pltpu.VMEM((tm, lru_w), F32),
                        pltpu.VMEM((tm, lru_w), F32),
                        pltpu.VMEM((SUBLANES, lru_w), F32)],
        compiler_params=pltpu.CompilerParams(
            dimension_semantics=("arbitrary", "arbitrary"), vmem_limit_bytes=VMEM_LIMIT_BYTES),
        name="mixer_in",
    )(h, cos_t, sin_t, *consts)


def _attn_kernel(qi_ref, kj_ref, q_ref, k_ref, vt_ref, o_ref, m_ref, acc_ref):
    tq = q_ref.shape[0]
    tk = k_ref.shape[0]
    heads = q_ref.shape[1] // HEAD_LANES
    p_idx = pl.program_id(1)
    qi = qi_ref[p_idx]
    kj = kj_ref[p_idx]
    key_off = kj * tk - qi * tq

    @pl.when(kj == 0)
    def _():
        m_ref[...] = jnp.full(m_ref.shape, MASK_VALUE, F32)
        acc_ref[...] = jnp.zeros_like(acc_ref)

    ones_rows = jnp.ones((ONES_ROWS, tk), BF16)

    def step(masked):
        if masked:
            key_chunk = (lax.broadcasted_iota(jnp.int32, (tk, tq), 0) + key_off) // CHUNK
            query_chunk = lax.broadcasted_iota(jnp.int32, (tk, tq), 1) // CHUNK
            visible = key_chunk <= query_chunk
        for h in range(heads):
            qh = q_ref[:, h * HEAD_LANES:(h + 1) * HEAD_LANES]
            kh = k_ref[:, h * HEAD_LANES:(h + 1) * HEAD_LANES]
            st = lax.dot_general(kh, qh, (((1,), (1,)), ((), ())), preferred_element_type=F32)
            if masked:
                st = jnp.where(visible, st, MASK_VALUE)
            m_prev = m_ref[h]
            m_new = jnp.maximum(m_prev, jnp.max(st, axis=0, keepdims=True))
            alpha = jnp.exp2(m_prev - m_new)
            pt = jnp.exp2(st - m_new).astype(BF16)
            m_ref[h] = m_new
            vt = jnp.concatenate([vt_ref[h * V_HEAD_DIM:(h + 1) * V_HEAD_DIM, :], ones_rows], axis=0)
            acc_ref[h] = acc_ref[h] * alpha + jnp.dot(vt, pt, preferred_element_type=F32)

    @pl.when(key_off + tk <= 0)
    def _():
        step(False)

    @pl.when(key_off + tk > 0)
    def _():
        step(True)

    @pl.when(key_off + tk == tq)
    def _():
        outs = []
        for h in range(heads):
            a = acc_ref[h]
            outs.append(a[:V_HEAD_DIM] / a[V_HEAD_DIM:V_HEAD_DIM + 1])
        o_ref[...] = jnp.concatenate(outs, axis=0).T.astype(o_ref.dtype)


def _attention(q, k, vt, batch):
    t = q.shape[0]
    seq = t // batch
    tq = min(ATTN_Q_TILE, seq)
    tk = min(ATTN_K_TILE, tq)
    assert tq % tk == 0 and tk % CHUNK == 0
    nq, nk = seq // tq, seq // tk
    heads = q.shape[1] // HEAD_LANES
    pairs = [(i, j) for i in range(nq) for j in range((i + 1) * (tq // tk))]
    qi = jnp.asarray(np.array([p[0] for p in pairs], np.int32))
    kj = jnp.asarray(np.array([p[1] for p in pairs], np.int32))
    return pl.pallas_call(
        _attn_kernel,
        grid_spec=pltpu.PrefetchScalarGridSpec(
            num_scalar_prefetch=2,
            grid=(batch, len(pairs)),
            in_specs=[pl.BlockSpec((tq, q.shape[1]), lambda b, p, qi_r, kj_r: (b * nq + qi_r[p], 0)),
                      pl.BlockSpec((tk, k.shape[1]), lambda b, p, qi_r, kj_r: (b * nk + kj_r[p], 0)),
                      pl.BlockSpec((vt.shape[0], tk), lambda b, p, qi_r, kj_r: (0, b * nk + kj_r[p]))],
            out_specs=pl.BlockSpec((tq, vt.shape[0]), lambda b, p, qi_r, kj_r: (b * nq + qi_r[p], 0)),
            scratch_shapes=[pltpu.VMEM((heads, 1, tq), F32),
                            pltpu.VMEM((heads, V_HEAD_DIM + ONES_ROWS, tq), F32)]),
        out_shape=jax.ShapeDtypeStruct((t, vt.shape[0]), BF16),
        compiler_params=pltpu.CompilerParams(
            dimension_semantics=("arbitrary", "arbitrary"), vmem_limit_bytes=VMEM_LIMIT_BYTES),
        name="attention",
    )(qi, kj, q, k, vt)


def _rot_half_cols(w):
    half = w.shape[-1] // 2
    return jnp.concatenate([-w[..., half:], w[..., :half]], axis=-1)


def _head_slab(nope, rope):
    rows, heads = rope.shape[0], rope.shape[1]
    if nope is None:
        nope = jnp.zeros((rows, heads, QK_NOPE_DIM), rope.dtype)
    pad = jnp.zeros((rows, heads, HEAD_LANES - QK_NOPE_DIM - QK_ROPE_DIM), rope.dtype)
    return jnp.concatenate([nope, rope, pad], axis=-1).reshape(rows, heads * HEAD_LANES)


def _block_diag_pairs(w_a, w_x):
    blocks, bd, _ = w_a.shape
    eye = jnp.eye(blocks, dtype=w_a.dtype)
    width = blocks * bd
    half = width // 2
    dense = lambda w: jnp.einsum('nde,nm->ndme', w, eye).reshape(width, width)
    da, dx = dense(w_a), dense(w_x)
    tiles = [jnp.concatenate([da[lo:lo + half, lo:lo + half], dx[lo:lo + half, lo:lo + half]], axis=1)
             for lo in (0, half)]
    return jnp.stack(tiles)


def kernel(x, positions, g_ffn1_pre, g_ffn1_post, w_ffn1_gate, w_ffn1_up, w_ffn1_down, g_mix_pre, g_mix_post, w_in, conv_w, conv_b, w_lru_a, b_lru_a, w_lru_x, b_lru_x, lru_lambda, q_a_norm, w_q_b, kv_a_norm, w_kv_b, w_out, g_ffn2_pre, g_ffn2_post, w_ffn2_gate, w_ffn2_up, w_ffn2_down):
    batch, seq, d = x.shape
    depth = g_ffn1_pre.shape[0]
    lru_w = conv_w.shape[-1]
    q_rank = q_a_norm.shape[-1]
    kv_rank = kv_a_norm.shape[-1]
    row = lambda v: v.reshape(1, -1)

    cos_t, sin_t = _rope_tables(positions)
    h = x.reshape(batch * seq, d)
    for l in range(depth):
        h = _ffn(h, row(g_ffn1_pre[l]), w_ffn1_gate[l].astype(BF16), w_ffn1_up[l].astype(BF16),
                 w_ffn1_down[l].astype(BF16), row(g_ffn1_post[l]))

        wi = w_in[l]
        c_rope = 2 * lru_w + q_rank + kv_rank
        k_rope_w = wi[:, c_rope:].reshape(d, 1, QK_ROPE_DIM)
        win = jnp.concatenate([wi[:, :c_rope], _head_slab(None, k_rope_w),
                               _head_slab(None, _rot_half_cols(k_rope_w))], axis=1).astype(BF16)
        wq = w_q_b[l].reshape(q_rank, MLA_HEADS, QK_NOPE_DIM + QK_ROPE_DIM)
        wqa = _head_slab(wq[..., :QK_NOPE_DIM], wq[..., QK_NOPE_DIM:]).astype(BF16)
        wqb = _head_slab(None, _rot_half_cols(wq[..., QK_NOPE_DIM:])).astype(BF16)
        wkv = w_kv_b[l].reshape(kv_rank, MLA_HEADS, QK_NOPE_DIM + V_HEAD_DIM)
        wk = _head_slab(wkv[..., :QK_NOPE_DIM], jnp.zeros((kv_rank, MLA_HEADS, QK_ROPE_DIM), F32)).astype(BF16)
        wv = wkv[..., QK_NOPE_DIM:].reshape(kv_rank, MLA_HEADS * V_HEAD_DIM).T.astype(BF16)
        wgate = _block_diag_pairs(w_lru_a[l], w_lru_x[l]).astype(BF16)

        q, k, v, y_lru = _mixer_in(h, cos_t, sin_t, batch, row(g_mix_pre[l]), win, conv_w[l], row(conv_b[l]),
                                   wgate, row(b_lru_a[l]), row(b_lru_x[l]), row(lru_lambda[l]),
                                   row(q_a_norm[l]), wqa, wqb, row(kv_a_norm[l]), wk, wv)
        y_mla = _attention(q, k, v, batch)
        h = _ffn(h, row(g_ffn2_pre[l]), w_ffn2_gate[l].astype(BF16), w_ffn2_up[l].astype(BF16),
                 w_ffn2_down[l].astype(BF16), row(g_ffn2_post[l]),
                 mix=(y_lru, y_mla, w_out[l].astype(BF16), row(g_mix_post[l])))
    return h.reshape(batch, seq, d)
```

```python
import functools
import math

import jax
import jax.numpy as jnp
from jax import lax
from jax.experimental import pallas as pl
from jax.experimental.pallas import tpu as pltpu

F32 = jnp.float32
BF16 = jnp.bfloat16

EPS = 1e-6
CHUNK = 64
LRU_C = 8.0
ROPE_THETA = 10000.0
MLA_HEADS = 8
QK_NOPE_DIM = 64
QK_ROPE_DIM = 32
V_HEAD_DIM = 64
HEAD_LANES = 128
MASK_VALUE = -1e30

VMEM_LIMIT_BYTES = 56 * 1024 * 1024
TOKEN_TILE = 512
SUBLANES = 8
LANES = 128
ONES_ROWS = 16
ROPE_TABLE_TILE = 2048
SCORES_AHEAD = 2
LOG2_E = 1.4426950408889634


def _rms(x, g):
    return x * lax.rsqrt(jnp.mean(x * x, axis=-1, keepdims=True) + EPS) * g


def _params(grid_rank, flags=None):
    return pltpu.CompilerParams(
        dimension_semantics=("arbitrary",) * grid_rank, vmem_limit_bytes=VMEM_LIMIT_BYTES, flags=flags)


def _const_spec(shape):
    nd = len(shape)
    return pl.BlockSpec(shape, lambda *_: (0,) * nd, pipeline_mode=pl.Buffered(1))


def _rope_table_kernel(pos_ref, invf_ref, cos_ref, sin_ref):
    ang = invf_ref[...] * pos_ref[...].astype(F32)
    cos_ref[...] = jnp.cos(ang)
    sin_ref[...] = jnp.sin(ang)


def _rope_tables(positions):
    half = QK_ROPE_DIM // 2
    t = positions.size
    inv_freq = 1.0 / (ROPE_THETA ** (jnp.arange(0, QK_ROPE_DIM, 2, dtype=F32) / QK_ROPE_DIM))
    tn = math.gcd(t, ROPE_TABLE_TILE)
    return pl.pallas_call(
        _rope_table_kernel,
        grid=(t // tn,),
        in_specs=[pl.BlockSpec((1, tn), lambda i: (0, i)), pl.BlockSpec((half, 1), lambda i: (0, 0))],
        out_specs=[pl.BlockSpec((half, tn), lambda i: (0, i))] * 2,
        out_shape=[jax.ShapeDtypeStruct((half, t), F32)] * 2,
        name="rope_tables",
    )(positions.reshape(1, t), inv_freq.reshape(half, 1))


def _ffn_kernel(*refs, with_mix_out, ff_chunks):
    if with_mix_out:
        (h_ref, ylru_ref, ymla_ref, wout_ref, gmix_ref,
         gpre_ref, wg_ref, wu_ref, wd_ref, gpost_ref, o_ref) = refs
        lru_w = ylru_ref.shape[1]
        m = jnp.dot(ylru_ref[...], wout_ref[:lru_w, :], preferred_element_type=F32)
        m = m + jnp.dot(ymla_ref[...], wout_ref[lru_w:, :], preferred_element_type=F32)
        h = h_ref[...] + _rms(m, gmix_ref[...])
    else:
        h_ref, gpre_ref, wg_ref, wu_ref, wd_ref, gpost_ref, o_ref = refs
        h = h_ref[...]
    n = _rms(h, gpre_ref[...]).astype(BF16)
    f = None
    for lo, hi in ff_chunks:
        g = jnp.dot(n, wg_ref[:, lo:hi], preferred_element_type=F32)
        u = jnp.dot(n, wu_ref[:, lo:hi], preferred_element_type=F32)
        a = (g * jax.nn.sigmoid(g) * u).astype(BF16)
        part = jnp.dot(a, wd_ref[lo:hi, :], preferred_element_type=F32)
        f = part if f is None else f + part
    o_ref[...] = h + 0.5 * _rms(f, gpost_ref[...])


def _ffn_chunks(d_ff, target=1536):
    chunks, lo = [], 0
    while lo < d_ff:
        hi = min(lo + target, d_ff)
        chunks.append((lo, hi))
        lo = hi
    return tuple(chunks)


def _ffn(h, gpre, wg, wu, wd, gpost, mix=None):
    t, d = h.shape
    d_ff = wg.shape[1]
    tm = min(TOKEN_TILE, t)
    tok = lambda w: pl.BlockSpec((tm, w), lambda i: (i, 0))
    ins, specs = [h], [tok(d)]
    if mix is not None:
        ylru, ymla, wout, gmix = mix
        ins += [ylru, ymla, wout, gmix]
        specs += [tok(ylru.shape[1]), tok(ymla.shape[1]), _const_spec(wout.shape), _const_spec(gmix.shape)]
    ins += [gpre, wg, wu, wd, gpost]
    specs += [_const_spec(a.shape) for a in (gpre, wg, wu, wd, gpost)]
    return pl.pallas_call(
        functools.partial(_ffn_kernel, with_mix_out=mix is not None, ff_chunks=_ffn_chunks(d_ff)),
        grid=(t // tm,),
        in_specs=specs,
        out_specs=tok(d),
        out_shape=jax.ShapeDtypeStruct((t, d), F32),
        compiler_params=_params(1),
        name="ffn_mix_out" if mix is not None else "ffn",
    )(*ins)


def _mixer_in_kernel(h_ref, cos_ref, sin_ref, gpre_ref, win_ref, convw_ref, convb_ref, wgate_ref,
                     ba_ref, bx_ref, lam_ref, qg_ref, wqa_ref, kvg_ref, wk_ref, wv_ref,
                     q_out, k_out, v_out, ylru_out, xbuf, hbuf, hcar, *, q_scale):
    tm = h_ref.shape[0]
    lru_w = ylru_out.shape[1]
    q_rank = qg_ref.shape[1]
    kv_rank = kvg_ref.shape[1]
    heads = q_out.shape[0] // HEAD_LANES
    nt_dims = (((1,), (1,)), ((), ()))

    @pl.when(pl.program_id(1) == 0)
    def _():
        xbuf[:, 0:SUBLANES, :] = jnp.zeros((xbuf.shape[0], SUBLANES, LANES), F32)
        hcar[...] = jnp.zeros_like(hcar)

    n = _rms(h_ref[...], gpre_ref[...]).astype(BF16)
    proj = jnp.dot(n, win_ref[...], preferred_element_type=F32)
    c0 = 2 * lru_w
    c1 = c0 + q_rank
    c2 = c1 + kv_rank
    xl = proj[:, :lru_w]
    gate = proj[:, lru_w:c0]
    kpe = proj[:, c2:c2 + HEAD_LANES]
    cos = cos_ref[...]
    sin = sin_ref[...]
    half_r = cos.shape[0]
    r_lo, r_mid, r_hi = QK_NOPE_DIM, QK_NOPE_DIM + half_r, QK_NOPE_DIM + 2 * half_r

    def rope_rows(slab):
        x1, x2 = slab[r_lo:r_mid], slab[r_mid:r_hi]
        return [slab[:r_lo], x1 * cos - x2 * sin, x2 * cos + x1 * sin, slab[r_hi:]]

    qn = _rms(proj[:, c0:c1], qg_ref[...]).astype(BF16)
    qa = lax.dot_general(wqa_ref[...], qn, nt_dims, preferred_element_type=F32)
    q_rows = []
    for hd in range(heads):
        q_rows += rope_rows(qa[hd * HEAD_LANES:(hd + 1) * HEAD_LANES])
    q_out[...] = (jnp.concatenate(q_rows, axis=0) * q_scale).astype(BF16)

    kvn = _rms(proj[:, c1:c2], kvg_ref[...]).astype(BF16)
    k_rope = jnp.concatenate(rope_rows(kpe.T), axis=0).T
    k_nope = jnp.dot(kvn, wk_ref[...], preferred_element_type=F32)
    k_out[...] = (k_nope + jnp.tile(k_rope, (1, heads))).astype(BF16)
    v_out[0] = lax.dot_general(wv_ref[...], kvn, nt_dims, preferred_element_type=F32).astype(BF16)

    slabs = lru_w // LANES
    seg_len = tm // SUBLANES
    pitch = seg_len + SUBLANES
    cw = convw_ref[...]
    cb = convb_ref[...]
    taps = cw.shape[0]
    xc_slabs = []
    for c in range(slabs):
        lanes = slice(c * LANES, (c + 1) * LANES)
        for s in range(SUBLANES):
            lo = SUBLANES + s * pitch
            xbuf[c, lo:lo + seg_len, :] = xl[s * seg_len:(s + 1) * seg_len, lanes]
            if s + 1 < SUBLANES:
                xbuf[c, lo + seg_len:lo + pitch, :] = xl[(s + 1) * seg_len - SUBLANES:(s + 1) * seg_len, lanes]
        x = [xbuf[c, pl.ds(SUBLANES + t, SUBLANES, stride=pitch), :] for t in range(1 - taps, seg_len)]
        xbuf[c, 0:SUBLANES, :] = xl[tm - SUBLANES:, lanes]
        w = [cw[k:k + 1, lanes] for k in range(taps)]
        steps = []
        for t in range(seg_len):
            acc = cb[:, lanes]
            for k in range(taps):
                acc = acc + x[t + k] * w[k]
            steps.append(acc)
        xc_slabs.append(jnp.concatenate(steps, axis=0))
    xc = jnp.concatenate(xc_slabs, axis=1)

    xcb = xc.astype(BF16)
    half = lru_w // 2
    g0 = jnp.dot(xcb[:, :half], wgate_ref[0], preferred_element_type=F32)
    g1 = jnp.dot(xcb[:, half:], wgate_ref[1], preferred_element_type=F32)
    r = jax.nn.sigmoid(jnp.concatenate([g0[:, :half], g1[:, :half]], axis=1) + ba_ref[...])
    i = jax.nn.sigmoid(jnp.concatenate([g0[:, half:], g1[:, half:]], axis=1) + bx_ref[...])
    log_a = (-LRU_C * r) * jax.nn.softplus(-lam_ref[...])
    a = jnp.exp(log_a)
    th = jnp.tanh(log_a)
    sq = -2.0 * th / (1.0 - th)
    u = jnp.where(sq > 0.0, sq * lax.rsqrt(sq), 0.0) * (i * xc)

    step = lambda t, v: v[t * SUBLANES:(t + 1) * SUBLANES]
    end, decay = step(0, u), step(0, a)
    for t in range(1, seg_len):
        end = step(t, a) * end + step(t, u)
        decay = step(t, a) * decay
    seg = lax.broadcasted_iota(jnp.int32, end.shape, 0)
    shift = 1
    while shift < SUBLANES:
        keep = seg >= shift
        end = jnp.where(keep, decay * pltpu.roll(end, shift, 0) + end, end)
        decay = jnp.where(keep, decay * pltpu.roll(decay, shift, 0), decay)
        shift *= 2
    h_in = hcar[...]
    state = jnp.where(seg == 0, h_in, pltpu.roll(end + decay * h_in, 1, 0))
    for t in range(seg_len):
        state = step(t, a) * state + step(t, u)
        for c in range(slabs):
            hbuf[c, pl.ds(t, SUBLANES, stride=pitch), :] = state[:, c * LANES:(c + 1) * LANES]
    hcar[...] = jnp.broadcast_to(state[SUBLANES - 1:SUBLANES, :], state.shape)
    h_lru = jnp.concatenate(
        [jnp.concatenate([hbuf[c, s * pitch:s * pitch + seg_len, :] for s in range(SUBLANES)], axis=0)
         for c in range(slabs)], axis=1)
    ylru_out[...] = (h_lru * jax.nn.gelu(gate)).astype(BF16)


def _mixer_in(h, cos_t, sin_t, batch, gpre, win, convw, convb, wgate, ba, bx, lam, qg, wqa, kvg, wk, wv):
    t, d = h.shape
    seq = t // batch
    tm = min(TOKEN_TILE, seq)
    ns = seq // tm
    lru_w = convw.shape[1]
    tok = lambda w: pl.BlockSpec((tm, w), lambda b, s: (b * ns + s, 0))
    cols = lambda r: pl.BlockSpec((r, tm), lambda b, s: (0, b * ns + s))
    consts = (gpre, win, convw, convb, wgate, ba, bx, lam, qg, wqa, kvg, wk, wv)
    q_scale = float((QK_NOPE_DIM + QK_ROPE_DIM) ** -0.5 * LOG2_E)
    return pl.pallas_call(
        functools.partial(_mixer_in_kernel, q_scale=q_scale),
        grid=(batch, ns),
        in_specs=[tok(d), cols(cos_t.shape[0]), cols(sin_t.shape[0])] + [_const_spec(a.shape) for a in consts],
        out_specs=[cols(wqa.shape[0]), tok(wk.shape[1]),
                   pl.BlockSpec((1, wv.shape[0], tm), lambda b, s: (b * ns + s, 0, 0)), tok(lru_w)],
        out_shape=[jax.ShapeDtypeStruct((wqa.shape[0], t), BF16),
                   jax.ShapeDtypeStruct((t, wk.shape[1]), BF16),
                   jax.ShapeDtypeStruct((t // tm, wv.shape[0], tm), BF16),
                   jax.ShapeDtypeStruct((t, lru_w), BF16)],
        scratch_shapes=[pltpu.VMEM((lru_w // LANES, tm + SUBLANES * SUBLANES + SUBLANES, LANES), F32),
                        pltpu.VMEM((lru_w // LANES, tm + SUBLANES * SUBLANES, LANES), F32),
                        pltpu.VMEM((SUBLANES, lru_w), F32)],
        compiler_params=_params(2),
        name="mixer_in",
    )(h, cos_t, sin_t, *consts)


def _attn_kernel(q_ref, k_ref, vt_ref, o_ref, m_ref, acc_ref, s_ref):
    tile = q_ref.shape[1]
    heads = q_ref.shape[0] // HEAD_LANES
    qi = pl.program_id(1)

    m_ref[...] = jnp.full(m_ref.shape, MASK_VALUE, F32)
    acc_ref[...] = jnp.zeros_like(acc_ref)
    ones_rows = jnp.ones((ONES_ROWS, tile), BF16)

    def step(kj, on_diagonal):
        key_rows = pl.ds(pl.multiple_of(kj * tile, tile), tile)
        if on_diagonal:
            key_chunk = lax.broadcasted_iota(jnp.int32, (tile, tile), 0) // CHUNK
            query_chunk = lax.broadcasted_iota(jnp.int32, (tile, tile), 1) // CHUNK
            visible = key_chunk <= query_chunk

        def scores(h):
            qh = q_ref[h * HEAD_LANES:(h + 1) * HEAD_LANES, :]
            kh = k_ref[key_rows, h * HEAD_LANES:(h + 1) * HEAD_LANES]
            st = jnp.dot(kh, qh, preferred_element_type=F32)
            if on_diagonal:
                st = jnp.where(visible, st, MASK_VALUE)
            s_ref[h] = st
            return jnp.max(s_ref[h], axis=0, keepdims=True)

        def accumulate(h, m_cur):
            m_prev = m_ref[h]
            m_new = jnp.maximum(m_prev, m_cur)
            alpha = jnp.exp2(m_prev - m_new)
            pt = jnp.exp2(s_ref[h] - m_new).astype(BF16)
            m_ref[h] = m_new
            vt = jnp.concatenate([vt_ref[kj, h * V_HEAD_DIM:(h + 1) * V_HEAD_DIM, :], ones_rows], axis=0)
            acc_ref[h] = acc_ref[h] * alpha + jnp.dot(vt, pt, preferred_element_type=F32)

        pending = [scores(h) for h in range(SCORES_AHEAD)]
        for h in range(heads):
            if h + SCORES_AHEAD < heads:
                pending.append(scores(h + SCORES_AHEAD))
            accumulate(h, pending.pop(0))

    def below_diagonal(kj, carry):
        step(kj, False)
        return carry

    lax.fori_loop(0, qi, below_diagonal, 0)
    step(qi, True)

    outs = []
    for h in range(heads):
        a = acc_ref[h]
        outs.append(a[:V_HEAD_DIM] / a[V_HEAD_DIM:V_HEAD_DIM + 1])
    o_ref[...] = jnp.concatenate(outs, axis=0).T.astype(o_ref.dtype)


def _attention(qt, k, vt, batch):
    t = k.shape[0]
    seq = t // batch
    tile = vt.shape[2]
    assert tile % CHUNK == 0 and seq % tile == 0
    nt = seq // tile
    heads = k.shape[1] // HEAD_LANES
    d_v = vt.shape[1]
    return pl.pallas_call(
        _attn_kernel,
        grid=(batch, nt),
        in_specs=[pl.BlockSpec((qt.shape[0], tile), lambda b, i: (0, b * nt + i)),
                  pl.BlockSpec((seq, k.shape[1]), lambda b, i: (b, 0)),
                  pl.BlockSpec((nt, d_v, tile), lambda b, i: (b, 0, 0))],
        out_specs=pl.BlockSpec((tile, d_v), lambda b, i: (b * nt + i, 0)),
        scratch_shapes=[pltpu.VMEM((heads, 1, tile), F32),
                        pltpu.VMEM((heads, V_HEAD_DIM + ONES_ROWS, tile), F32),
                        pltpu.VMEM((heads, tile, tile), F32)],
        out_shape=jax.ShapeDtypeStruct((t, d_v), BF16),
        compiler_params=_params(2),
        name="attention",
    )(qt, k, vt)


def _head_slab(nope, rope):
    rows, heads = rope.shape[0], rope.shape[1]
    if nope is None:
        nope = jnp.zeros((rows, heads, QK_NOPE_DIM), rope.dtype)
    pad = jnp.zeros((rows, heads, HEAD_LANES - QK_NOPE_DIM - QK_ROPE_DIM), rope.dtype)
    return jnp.concatenate([nope, rope, pad], axis=-1).reshape(rows, heads * HEAD_LANES)


def _block_diag_pairs(w_a, w_x):
    blocks, bd, _ = w_a.shape
    eye = jnp.eye(blocks, dtype=w_a.dtype)
    width = blocks * bd
    half = width // 2
    dense = lambda w: jnp.einsum('nde,nm->ndme', w, eye).reshape(width, width)
    da, dx = dense(w_a), dense(w_x)
    tiles = [jnp.concatenate([da[lo:lo + half, lo:lo + half], dx[lo:lo + half, lo:lo + half]], axis=1)
             for lo in (0, half)]
    return jnp.stack(tiles)


def kernel(x, positions, g_ffn1_pre, g_ffn1_post, w_ffn1_gate, w_ffn1_up, w_ffn1_down, g_mix_pre, g_mix_post, w_in, conv_w, conv_b, w_lru_a, b_lru_a, w_lru_x, b_lru_x, lru_lambda, q_a_norm, w_q_b, kv_a_norm, w_kv_b, w_out, g_ffn2_pre, g_ffn2_post, w_ffn2_gate, w_ffn2_up, w_ffn2_down):
    batch, seq, d = x.shape
    depth = g_ffn1_pre.shape[0]
    lru_w = conv_w.shape[-1]
    q_rank = q_a_norm.shape[-1]
    kv_rank = kv_a_norm.shape[-1]
    row = lambda v: v.reshape(1, -1)

    cos_t, sin_t = _rope_tables(positions)
    h = x.reshape(batch * seq, d)
    for l in range(depth):
        h = _ffn(h, row(g_ffn1_pre[l]), w_ffn1_gate[l].astype(BF16), w_ffn1_up[l].astype(BF16),
                 w_ffn1_down[l].astype(BF16), row(g_ffn1_post[l]))

        wi = w_in[l]
        c_rope = 2 * lru_w + q_rank + kv_rank
        k_rope_w = wi[:, c_rope:].reshape(d, 1, QK_ROPE_DIM)
        win = jnp.concatenate([wi[:, :c_rope], _head_slab(None, k_rope_w)], axis=1).astype(BF16)
        wq = w_q_b[l].reshape(q_rank, MLA_HEADS, QK_NOPE_DIM + QK_ROPE_DIM)
        wqa = _head_slab(wq[..., :QK_NOPE_DIM], wq[..., QK_NOPE_DIM:]).T.astype(BF16)
        wkv = w_kv_b[l].reshape(kv_rank, MLA_HEADS, QK_NOPE_DIM + V_HEAD_DIM)
        wk = _head_slab(wkv[..., :QK_NOPE_DIM], jnp.zeros((kv_rank, MLA_HEADS, QK_ROPE_DIM), F32)).astype(BF16)
        wv = wkv[..., QK_NOPE_DIM:].reshape(kv_rank, MLA_HEADS * V_HEAD_DIM).T.astype(BF16)
        wgate = _block_diag_pairs(w_lru_a[l], w_lru_x[l]).astype(BF16)

        q, k, v, y_lru = _mixer_in(h, cos_t, sin_t, batch, row(g_mix_pre[l]), win, conv_w[l], row(conv_b[l]),
                                   wgate, row(b_lru_a[l]), row(b_lru_x[l]), row(lru_lambda[l]),
                                   row(q_a_norm[l]), wqa, row(kv_a_norm[l]), wk, wv)
        y_mla = _attention(q, k, v, batch)
        h = _ffn(h, row(g_ffn2_pre[l]), w_ffn2_gate[l].astype(BF16), w_ffn2_up[l].astype(BF16),
                 w_ffn2_down[l].astype(BF16), row(g_ffn2_post[l]),
                 mix=(y_lru, y_mla, w_out[l].astype(BF16), row(g_mix_post[l])))
    return h.reshape(batch, seq, d)
```

```python
import functools
import math

import jax
import jax.numpy as jnp
from jax import lax
from jax.experimental import pallas as pl
from jax.experimental.pallas import tpu as pltpu

F32 = jnp.float32
BF16 = jnp.bfloat16

EPS = 1e-6
CHUNK = 64
LRU_C = 8.0
ROPE_THETA = 10000.0
MLA_HEADS = 8
QK_NOPE_DIM = 64
QK_ROPE_DIM = 32
V_HEAD_DIM = 64
HEAD_LANES = 128
MASK_VALUE = -1e30

VMEM_LIMIT_BYTES = 56 * 1024 * 1024
TOKEN_TILE = 512
SUBLANES = 8
LANES = 128
ONES_ROWS = 16
ROPE_TABLE_TILE = 2048
SCORES_AHEAD = 2
LOG2_E = 1.4426950408889634


def _rms(x, g):
    return x * lax.rsqrt(jnp.mean(x * x, axis=-1, keepdims=True) + EPS) * g


def _params(grid_rank, flags=None):
    return pltpu.CompilerParams(
        dimension_semantics=("arbitrary",) * grid_rank, vmem_limit_bytes=VMEM_LIMIT_BYTES, flags=flags)


def _const_spec(shape):
    nd = len(shape)
    return pl.BlockSpec(shape, lambda *_: (0,) * nd, pipeline_mode=pl.Buffered(1))


def _rope_table_kernel(pos_ref, invf_ref, cos_ref, sin_ref):
    ang = invf_ref[...] * pos_ref[...].astype(F32)
    cos_ref[...] = jnp.cos(ang)
    sin_ref[...] = jnp.sin(ang)


def _rope_tables(positions):
    half = QK_ROPE_DIM // 2
    t = positions.size
    inv_freq = 1.0 / (ROPE_THETA ** (jnp.arange(0, QK_ROPE_DIM, 2, dtype=F32) / QK_ROPE_DIM))
    tn = math.gcd(t, ROPE_TABLE_TILE)
    return pl.pallas_call(
        _rope_table_kernel,
        grid=(t // tn,),
        in_specs=[pl.BlockSpec((1, tn), lambda i: (0, i)), pl.BlockSpec((half, 1), lambda i: (0, 0))],
        out_specs=[pl.BlockSpec((half, tn), lambda i: (0, i))] * 2,
        out_shape=[jax.ShapeDtypeStruct((half, t), F32)] * 2,
        name="rope_tables",
    )(positions.reshape(1, t), inv_freq.reshape(half, 1))


def _ffn_kernel(*refs, with_mix_out, ff_chunks):
    if with_mix_out:
        (h_ref, ylru_ref, ymla_ref, wout_ref, gmix_ref,
         gpre_ref, wg_ref, wu_ref, wd_ref, gpost_ref, o_ref) = refs
        lru_w = ylru_ref.shape[1]
        m = jnp.dot(ylru_ref[...], wout_ref[:lru_w, :], preferred_element_type=F32)
        m = m + jnp.dot(ymla_ref[...], wout_ref[lru_w:, :], preferred_element_type=F32)
        h = h_ref[...] + _rms(m, gmix_ref[...])
    else:
        h_ref, gpre_ref, wg_ref, wu_ref, wd_ref, gpost_ref, o_ref = refs
        h = h_ref[...]
    n = _rms(h, gpre_ref[...]).astype(BF16)
    f = None
    for lo, hi in ff_chunks:
        g = jnp.dot(n, wg_ref[:, lo:hi], preferred_element_type=F32)
        u = jnp.dot(n, wu_ref[:, lo:hi], preferred_element_type=F32)
        a = (g * jax.nn.sigmoid(g) * u).astype(BF16)
        part = jnp.dot(a, wd_ref[lo:hi, :], preferred_element_type=F32)
        f = part if f is None else f + part
    o_ref[...] = h + 0.5 * _rms(f, gpost_ref[...])


def _ffn_chunks(d_ff, target=1536):
    chunks, lo = [], 0
    while lo < d_ff:
        hi = min(lo + target, d_ff)
        chunks.append((lo, hi))
        lo = hi
    return tuple(chunks)


def _ffn(h, gpre, wg, wu, wd, gpost, mix=None):
    t, d = h.shape
    d_ff = wg.shape[1]
    tm = min(TOKEN_TILE, t)
    tok = lambda w: pl.BlockSpec((tm, w), lambda i: (i, 0))
    ins, specs = [h], [tok(d)]
    if mix is not None:
        ylru, ymla, wout, gmix = mix
        ins += [ylru, ymla, wout, gmix]
        specs += [tok(ylru.shape[1]), tok(ymla.shape[1]), _const_spec(wout.shape), _const_spec(gmix.shape)]
    ins += [gpre, wg, wu, wd, gpost]
    specs += [_const_spec(a.shape) for a in (gpre, wg, wu, wd, gpost)]
    return pl.pallas_call(
        functools.partial(_ffn_kernel, with_mix_out=mix is not None, ff_chunks=_ffn_chunks(d_ff)),
        grid=(t // tm,),
        in_specs=specs,
        out_specs=tok(d),
        out_shape=jax.ShapeDtypeStruct((t, d), F32),
        compiler_params=_params(1),
        name="ffn_mix_out" if mix is not None else "ffn",
    )(*ins)


def _mixer_in_kernel(h_ref, cos_ref, sin_ref, gpre_ref, win_ref, convw_ref, convb_ref, wgate_ref,
                     ba_ref, bx_ref, lam_ref, qg_ref, wqa_ref, kvg_ref, wk_ref, wv_ref,
                     q_out, k_out, v_out, ylru_out, xbuf, hbuf, hcar, *, q_scale):
    tm = h_ref.shape[0]
    lru_w = ylru_out.shape[1]
    q_rank = qg_ref.shape[1]
    kv_rank = kvg_ref.shape[1]
    heads = q_out.shape[0] // HEAD_LANES
    nt_dims = (((1,), (1,)), ((), ()))

    @pl.when(pl.program_id(1) == 0)
    def _():
        xbuf[:, 0:SUBLANES, :] = jnp.zeros((xbuf.shape[0], SUBLANES, LANES), F32)
        hcar[...] = jnp.zeros_like(hcar)

    n = _rms(h_ref[...], gpre_ref[...]).astype(BF16)
    proj = jnp.dot(n, win_ref[...], preferred_element_type=F32)
    c0 = 2 * lru_w
    c1 = c0 + q_rank
    c2 = c1 + kv_rank
    xl = proj[:, :lru_w]
    gate = proj[:, lru_w:c0]
    kpe = proj[:, c2:c2 + HEAD_LANES]
    cos = cos_ref[...]
    sin = sin_ref[...]
    half_r = cos.shape[0]
    r_lo, r_mid, r_hi = QK_NOPE_DIM, QK_NOPE_DIM + half_r, QK_NOPE_DIM + 2 * half_r

    def rope_rows(slab):
        x1, x2 = slab[r_lo:r_mid], slab[r_mid:r_hi]
        return [slab[:r_lo], x1 * cos - x2 * sin, x2 * cos + x1 * sin, slab[r_hi:]]

    qn = _rms(proj[:, c0:c1], qg_ref[...]).astype(BF16)
    qa = lax.dot_general(wqa_ref[...], qn, nt_dims, preferred_element_type=F32)
    q_rows = []
    for hd in range(heads):
        q_rows += rope_rows(qa[hd * HEAD_LANES:(hd + 1) * HEAD_LANES])
    q_out[...] = (jnp.concatenate(q_rows, axis=0) * q_scale).astype(BF16)

    kvn = _rms(proj[:, c1:c2], kvg_ref[...]).astype(BF16)
    k_rope = jnp.concatenate(rope_rows(kpe.T), axis=0).T
    k_nope = jnp.dot(kvn, wk_ref[...], preferred_element_type=F32)
    k_out[...] = (k_nope + jnp.tile(k_rope, (1, heads))).astype(BF16)
    v_out[0] = lax.dot_general(wv_ref[...], kvn, nt_dims, preferred_element_type=F32).astype(BF16)

    slabs = lru_w // LANES
    seg_len = tm // SUBLANES
    pitch = seg_len + SUBLANES
    cw = convw_ref[...]
    cb = convb_ref[...]
    taps = cw.shape[0]
    xc_slabs = []
    for c in range(slabs):
        lanes = slice(c * LANES, (c + 1) * LANES)
        for s in range(SUBLANES):
            lo = SUBLANES + s * pitch
            xbuf[c, lo:lo + seg_len, :] = xl[s * seg_len:(s + 1) * seg_len, lanes]
            if s + 1 < SUBLANES:
                xbuf[c, lo + seg_len:lo + pitch, :] = xl[(s + 1) * seg_len - SUBLANES:(s + 1) * seg_len, lanes]
        x = [xbuf[c, pl.ds(SUBLANES + t, SUBLANES, stride=pitch), :] for t in range(1 - taps, seg_len)]
        xbuf[c, 0:SUBLANES, :] = xl[tm - SUBLANES:, lanes]
        w = [cw[k:k + 1, lanes] for k in range(taps)]
        steps = []
        for t in range(seg_len):
            acc = cb[:, lanes]
            for k in range(taps):
                acc = acc + x[t + k] * w[k]
            steps.append(acc)
        xc_slabs.append(jnp.concatenate(steps, axis=0))
    xc = jnp.concatenate(xc_slabs, axis=1)

    xcb = xc.astype(BF16)
    half = lru_w // 2
    g0 = jnp.dot(xcb[:, :half], wgate_ref[0], preferred_element_type=F32)
    g1 = jnp.dot(xcb[:, half:], wgate_ref[1], preferred_element_type=F32)
    r = jax.nn.sigmoid(jnp.concatenate([g0[:, :half], g1[:, :half]], axis=1) + ba_ref[...])
    i = jax.nn.sigmoid(jnp.concatenate([g0[:, half:], g1[:, half:]], axis=1) + bx_ref[...])
    log_a = (-LRU_C * r) * jax.nn.softplus(-lam_ref[...])
    a = jnp.exp(log_a)
    th = jnp.tanh(log_a)
    sq = -2.0 * th / (1.0 - th)
    u = jnp.where(sq > 0.0, sq * lax.rsqrt(sq), 0.0) * (i * xc)

    step = lambda t, v: v[t * SUBLANES:(t + 1) * SUBLANES]
    end, decay = step(0, u), step(0, a)
    for t in range(1, seg_len):
        end = step(t, a) * end + step(t, u)
        decay = step(t, a) * decay
    seg = lax.broadcasted_iota(jnp.int32, end.shape, 0)
    shift = 1
    while shift < SUBLANES:
        keep = seg >= shift
        end = jnp.where(keep, decay * pltpu.roll(end, shift, 0) + end, end)
        decay = jnp.where(keep, decay * pltpu.roll(decay, shift, 0), decay)
        shift *= 2
    h_in = hcar[...]
    state = jnp.where(seg == 0, h_in, pltpu.roll(end + decay * h_in, 1, 0))
    for t in range(seg_len):
        state = step(t, a) * state + step(t, u)
        for c in range(slabs):
            hbuf[c, pl.ds(t, SUBLANES, stride=pitch), :] = state[:, c * LANES:(c + 1) * LANES]
    hcar[...] = jnp.broadcast_to(state[SUBLANES - 1:SUBLANES, :], state.shape)
    h_lru = jnp.concatenate(
        [jnp.concatenate([hbuf[c, s * pitch:s * pitch + seg_len, :] for s in range(SUBLANES)], axis=0)
         for c in range(slabs)], axis=1)
    ylru_out[...] = (h_lru * jax.nn.gelu(gate)).astype(BF16)


def _mixer_in(h, cos_t, sin_t, batch, gpre, win, convw, convb, wgate, ba, bx, lam, qg, wqa, kvg, wk, wv):
    t, d = h.shape
    seq = t // batch
    tm = min(TOKEN_TILE, seq)
    ns = seq // tm
    lru_w = convw.shape[1]
    tok = lambda w: pl.BlockSpec((tm, w), lambda b, s: (b * ns + s, 0))
    cols = lambda r: pl.BlockSpec((r, tm), lambda b, s: (0, b * ns + s))
    consts = (gpre, win, convw, convb, wgate, ba, bx, lam, qg, wqa, kvg, wk, wv)
    q_scale = float((QK_NOPE_DIM + QK_ROPE_DIM) ** -0.5 * LOG2_E)
    return pl.pallas_call(
        functools.partial(_mixer_in_kernel, q_scale=q_scale),
        grid=(batch, ns),
        in_specs=[tok(d), cols(cos_t.shape[0]), cols(sin_t.shape[0])] + [_const_spec(a.shape) for a in consts],
        out_specs=[cols(wqa.shape[0]), tok(wk.shape[1]),
                   pl.BlockSpec((1, wv.shape[0], tm), lambda b, s: (b * ns + s, 0, 0)), tok(lru_w)],
        out_shape=[jax.ShapeDtypeStruct((wqa.shape[0], t), BF16),
                   jax.ShapeDtypeStruct((t, wk.shape[1]), BF16),
                   jax.ShapeDtypeStruct((t // tm, wv.shape[0], tm), BF16),
                   jax.ShapeDtypeStruct((t, lru_w), BF16)],
        scratch_shapes=[pltpu.VMEM((lru_w // LANES, tm + SUBLANES * SUBLANES + SUBLANES, LANES), F32),
                        pltpu.VMEM((lru_w // LANES, tm + SUBLANES * SUBLANES, LANES), F32),
                        pltpu.VMEM((SUBLANES, lru_w), F32)],
        compiler_params=_params(2),
        name="mixer_in",
    )(h, cos_t, sin_t, *consts)


def _attn_kernel(q_ref, k_ref, vt_ref, o_ref, m_ref, acc_ref, s_ref):
    tile = q_ref.shape[1]
    half = tile // 2
    heads = q_ref.shape[0] // HEAD_LANES
    qi = pl.program_id(1)

    m_ref[...] = jnp.full(m_ref.shape, MASK_VALUE, F32)
    acc_ref[...] = jnp.zeros_like(acc_ref)
    ones_rows = jnp.ones((ONES_ROWS, tile), BF16)
    slab = lambda h: slice(h * HEAD_LANES, (h + 1) * HEAD_LANES)

    def v_rows(kj, h):
        return jnp.concatenate([vt_ref[kj, h * V_HEAD_DIM:(h + 1) * V_HEAD_DIM, :], ones_rows], axis=0)

    def rescale(h, m_cur):
        m_prev = m_ref[h]
        m_new = jnp.maximum(m_prev, m_cur)
        m_ref[h] = m_new
        return m_new, jnp.exp2(m_prev - m_new)

    def full_scores(kj, h, slot):
        kh = k_ref[pl.ds(pl.multiple_of(kj * tile, tile), tile), slab(h)]
        s_ref[slot] = jnp.dot(kh, q_ref[slab(h), :], preferred_element_type=F32)
        return jnp.max(s_ref[slot], axis=0, keepdims=True)

    def full_accumulate(kj, h, slot, m_cur):
        m_new, alpha = rescale(h, m_cur)
        pt = jnp.exp2(s_ref[slot] - m_new).astype(BF16)
        acc_ref[h] = acc_ref[h] * alpha + jnp.dot(v_rows(kj, h), pt, preferred_element_type=F32)

    def diag_scores(kj, h, slot):
        key_chunk = lax.broadcasted_iota(jnp.int32, (half, half), 0) // CHUNK
        query_chunk = lax.broadcasted_iota(jnp.int32, (half, half), 1) // CHUNK
        visible = key_chunk <= query_chunk
        row0 = pl.multiple_of(kj * tile, tile)
        qh = q_ref[slab(h), :]
        early = jnp.dot(k_ref[pl.ds(row0, half), slab(h)], qh, preferred_element_type=F32)
        late = jnp.dot(k_ref[pl.ds(row0 + half, half), slab(h)], qh[:, half:], preferred_element_type=F32)
        s_ref[slot, :half, :half] = jnp.where(visible, early[:, :half], MASK_VALUE)
        s_ref[slot, :half, half:] = early[:, half:]
        s_ref[slot, half:, half:] = jnp.where(visible, late, MASK_VALUE)
        m_left = jnp.max(s_ref[slot, :half, :half], axis=0, keepdims=True)
        m_right = jnp.maximum(jnp.max(s_ref[slot, :half, half:], axis=0, keepdims=True),
                              jnp.max(s_ref[slot, half:, half:], axis=0, keepdims=True))
        return jnp.concatenate([m_left, m_right], axis=1)

    def diag_accumulate(kj, h, slot, m_cur):
        m_new, alpha = rescale(h, m_cur)
        p_early = jnp.exp2(s_ref[slot, :half, :] - m_new).astype(BF16)
        p_late = jnp.exp2(s_ref[slot, half:, half:] - m_new[:, half:]).astype(BF16)
        vt = v_rows(kj, h)
        pv = jnp.dot(vt[:, :half], p_early, preferred_element_type=F32)
        pv_late = jnp.dot(vt[:, half:], p_late, preferred_element_type=F32)
        acc = acc_ref[h] * alpha + pv
        acc_ref[h] = jnp.concatenate([acc[:, :half], acc[:, half:] + pv_late], axis=1)

    def run(tiles):
        units = [(kj, h, diag) for kj, diag in tiles for h in range(heads)]

        def issue(n):
            kj, h, diag = units[n]
            return (diag_scores if diag else full_scores)(kj, h, n % heads)

        pending = [issue(n) for n in range(min(SCORES_AHEAD, len(units)))]
        for n, (kj, h, diag) in enumerate(units):
            if n + SCORES_AHEAD < len(units):
                pending.append(issue(n + SCORES_AHEAD))
            (diag_accumulate if diag else full_accumulate)(kj, h, n % heads, pending.pop(0))

    def tile_pair(j, carry):
        run([(2 * j, False), (2 * j + 1, False)])
        return carry

    lax.fori_loop(0, qi // 2, tile_pair, 0)

    @pl.when(qi % 2 == 1)
    def _():
        run([(qi - 1, False)])

    run([(qi, True)])

    outs = []
    for h in range(heads):
        a = acc_ref[h]
        outs.append(a[:V_HEAD_DIM] / a[V_HEAD_DIM:V_HEAD_DIM + 1])
    o_ref[...] = jnp.concatenate(outs, axis=0).T.astype(o_ref.dtype)


def _attention(qt, k, vt, batch):
    t = k.shape[0]
    seq = t // batch
    tile = vt.shape[2]
    assert tile % (2 * CHUNK) == 0 and seq % tile == 0
    nt = seq // tile
    heads = k.shape[1] // HEAD_LANES
    d_v = vt.shape[1]
    return pl.pallas_call(
        _attn_kernel,
        grid=(batch, nt),
        in_specs=[pl.BlockSpec((qt.shape[0], tile), lambda b, i: (0, b * nt + i)),
                  pl.BlockSpec((seq, k.shape[1]), lambda b, i: (b, 0)),
                  pl.BlockSpec((nt, d_v, tile), lambda b, i: (b, 0, 0))],
        out_specs=pl.BlockSpec((tile, d_v), lambda b, i: (b * nt + i, 0)),
        scratch_shapes=[pltpu.VMEM((heads, 1, tile), F32),
                        pltpu.VMEM((heads, V_HEAD_DIM + ONES_ROWS, tile), F32),
                        pltpu.VMEM((heads, tile, tile), F32)],
        out_shape=jax.ShapeDtypeStruct((t, d_v), BF16),
        compiler_params=_params(2),
        name="attention",
    )(qt, k, vt)


def _head_slab(nope, rope):
    rows, heads = rope.shape[0], rope.shape[1]
    if nope is None:
        nope = jnp.zeros((rows, heads, QK_NOPE_DIM), rope.dtype)
    pad = jnp.zeros((rows, heads, HEAD_LANES - QK_NOPE_DIM - QK_ROPE_DIM), rope.dtype)
    return jnp.concatenate([nope, rope, pad], axis=-1).reshape(rows, heads * HEAD_LANES)


def _block_diag_pairs(w_a, w_x):
    blocks, bd, _ = w_a.shape
    eye = jnp.eye(blocks, dtype=w_a.dtype)
    width = blocks * bd
    half = width // 2
    dense = lambda w: jnp.einsum('nde,nm->ndme', w, eye).reshape(width, width)
    da, dx = dense(w_a), dense(w_x)
    tiles = [jnp.concatenate([da[lo:lo + half, lo:lo + half], dx[lo:lo + half, lo:lo + half]], axis=1)
             for lo in (0, half)]
    return jnp.stack(tiles)


def kernel(x, positions, g_ffn1_pre, g_ffn1_post, w_ffn1_gate, w_ffn1_up, w_ffn1_down, g_mix_pre, g_mix_post, w_in, conv_w, conv_b, w_lru_a, b_lru_a, w_lru_x, b_lru_x, lru_lambda, q_a_norm, w_q_b, kv_a_norm, w_kv_b, w_out, g_ffn2_pre, g_ffn2_post, w_ffn2_gate, w_ffn2_up, w_ffn2_down):
    batch, seq, d = x.shape
    depth = g_ffn1_pre.shape[0]
    lru_w = conv_w.shape[-1]
    q_rank = q_a_norm.shape[-1]
    kv_rank = kv_a_norm.shape[-1]
    row = lambda v: v.reshape(1, -1)

    cos_t, sin_t = _rope_tables(positions)
    h = x.reshape(batch * seq, d)
    for l in range(depth):
        h = _ffn(h, row(g_ffn1_pre[l]), w_ffn1_gate[l].astype(BF16), w_ffn1_up[l].astype(BF16),
                 w_ffn1_down[l].astype(BF16), row(g_ffn1_post[l]))

        wi = w_in[l]
        c_rope = 2 * lru_w + q_rank + kv_rank
        k_rope_w = wi[:, c_rope:].reshape(d, 1, QK_ROPE_DIM)
        win = jnp.concatenate([wi[:, :c_rope], _head_slab(None, k_rope_w)], axis=1).astype(BF16)
        wq = w_q_b[l].reshape(q_rank, MLA_HEADS, QK_NOPE_DIM + QK_ROPE_DIM)
        wqa = _head_slab(wq[..., :QK_NOPE_DIM], wq[..., QK_NOPE_DIM:]).T.astype(BF16)
        wkv = w_kv_b[l].reshape(kv_rank, MLA_HEADS, QK_NOPE_DIM + V_HEAD_DIM)
        wk = _head_slab(wkv[..., :QK_NOPE_DIM], jnp.zeros((kv_rank, MLA_HEADS, QK_ROPE_DIM), F32)).astype(BF16)
        wv = wkv[..., QK_NOPE_DIM:].reshape(kv_rank, MLA_HEADS * V_HEAD_DIM).T.astype(BF16)
        wgate = _block_diag_pairs(w_lru_a[l], w_lru_x[l]).astype(BF16)

        q, k, v, y_lru = _mixer_in(h, cos_t, sin_t, batch, row(g_mix_pre[l]), win, conv_w[l], row(conv_b[l]),
                                   wgate, row(b_lru_a[l]), row(b_lru_x[l]), row(lru_lambda[l]),
                                   row(q_a_norm[l]), wqa, row(kv_a_norm[l]), wk, wv)
        y_mla = _attention(q, k, v, batch)
        h = _ffn(h, row(g_ffn2_pre[l]), w_ffn2_gate[l].astype(BF16), w_ffn2_up[l].astype(BF16),
                 w_ffn2_down[l].astype(BF16), row(g_ffn2_post[l]),
                 mix=(y_lru, y_mla, w_out[l].astype(BF16), row(g_mix_post[l])))
    return h.reshape(batch, seq, d)
```

```python
import functools
import math

import jax
import jax.numpy as jnp
from jax import lax
from jax.experimental import pallas as pl
from jax.experimental.pallas import tpu as pltpu

F32 = jnp.float32
BF16 = jnp.bfloat16

EPS = 1e-6
CHUNK = 64
LRU_C = 8.0
ROPE_THETA = 10000.0
MLA_HEADS = 8
QK_NOPE_DIM = 64
QK_ROPE_DIM = 32
V_HEAD_DIM = 64
HEAD_LANES = 128
MASK_VALUE = -1e30

VMEM_LIMIT_BYTES = 56 * 1024 * 1024
TOKEN_TILE = 512
FFN_TOKEN_TILE = 1024
FFN_ROW_GROUPS = 4
MIXER_ROW_GROUPS = 1
SUBLANES = 8
LANES = 128
ONES_ROWS = 16
ROPE_TABLE_TILE = 2048
SCORES_AHEAD = 2
LOG2_E = 1.4426950408889634


def _rms(x, g):
    return x * lax.rsqrt(jnp.mean(x * x, axis=-1, keepdims=True) + EPS) * g


def _params(grid_rank, flags=None):
    return pltpu.CompilerParams(
        dimension_semantics=("arbitrary",) * grid_rank, vmem_limit_bytes=VMEM_LIMIT_BYTES, flags=flags)


def _const_spec(shape):
    nd = len(shape)
    return pl.BlockSpec(shape, lambda *_: (0,) * nd, pipeline_mode=pl.Buffered(1))


def _rope_table_kernel(pos_ref, invf_ref, cos_ref, sin_ref):
    ang = invf_ref[...] * pos_ref[...].astype(F32)
    cos_ref[...] = jnp.cos(ang)
    sin_ref[...] = jnp.sin(ang)


def _rope_tables(positions):
    half = QK_ROPE_DIM // 2
    t = positions.size
    inv_freq = 1.0 / (ROPE_THETA ** (jnp.arange(0, QK_ROPE_DIM, 2, dtype=F32) / QK_ROPE_DIM))
    tn = math.gcd(t, ROPE_TABLE_TILE)
    return pl.pallas_call(
        _rope_table_kernel,
        grid=(t // tn,),
        in_specs=[pl.BlockSpec((1, tn), lambda i: (0, i)), pl.BlockSpec((half, 1), lambda i: (0, 0))],
        out_specs=[pl.BlockSpec((half, tn), lambda i: (0, i))] * 2,
        out_shape=[jax.ShapeDtypeStruct((half, t), F32)] * 2,
        name="rope_tables",
    )(positions.reshape(1, t), inv_freq.reshape(half, 1))


def _ffn_kernel(*refs, with_mix_out, ff_chunks):
    if with_mix_out:
        (h_ref, ylru_ref, ymla_ref, wout_ref, gmix_ref,
         gpre_ref, wg_ref, wu_ref, wd_ref, gpost_ref, o_ref) = refs
    else:
        h_ref, gpre_ref, wg_ref, wu_ref, wd_ref, gpost_ref, o_ref = refs
    tm = h_ref.shape[0]
    groups = [slice(r0, r0 + tm // FFN_ROW_GROUPS) for r0 in range(0, tm, tm // FFN_ROW_GROUPS)]
    hs = []
    for rows in groups:
        h = h_ref[rows, :]
        if with_mix_out:
            lru_w = ylru_ref.shape[1]
            m = jnp.dot(ylru_ref[rows, :], wout_ref[:lru_w, :], preferred_element_type=F32)
            m = m + jnp.dot(ymla_ref[rows, :], wout_ref[lru_w:, :], preferred_element_type=F32)
            h = h + _rms(m, gmix_ref[...])
        hs.append(h)
    for rows, h in zip(groups, hs):
        n = _rms(h, gpre_ref[...]).astype(BF16)
        f = None
        for lo, hi in ff_chunks:
            g = jnp.dot(n, wg_ref[:, lo:hi], preferred_element_type=F32)
            u = jnp.dot(n, wu_ref[:, lo:hi], preferred_element_type=F32)
            a = (g * jax.nn.sigmoid(g) * u).astype(BF16)
            part = jnp.dot(a, wd_ref[lo:hi, :], preferred_element_type=F32)
            f = part if f is None else f + part
        o_ref[rows, :] = h + 0.5 * _rms(f, gpost_ref[...])


def _ffn_chunks(d_ff, target=1536):
    chunks, lo = [], 0
    while lo < d_ff:
        hi = min(lo + target, d_ff)
        chunks.append((lo, hi))
        lo = hi
    return tuple(chunks)


def _ffn(h, gpre, wg, wu, wd, gpost, mix=None):
    t, d = h.shape
    d_ff = wg.shape[1]
    tm = math.gcd(FFN_TOKEN_TILE, t)
    tok = lambda w: pl.BlockSpec((tm, w), lambda i: (i, 0))
    ins, specs = [h], [tok(d)]
    if mix is not None:
        ylru, ymla, wout, gmix = mix
        ins += [ylru, ymla, wout, gmix]
        specs += [tok(ylru.shape[1]), tok(ymla.shape[1]), _const_spec(wout.shape), _const_spec(gmix.shape)]
    ins += [gpre, wg, wu, wd, gpost]
    specs += [_const_spec(a.shape) for a in (gpre, wg, wu, wd, gpost)]
    return pl.pallas_call(
        functools.partial(_ffn_kernel, with_mix_out=mix is not None, ff_chunks=_ffn_chunks(d_ff)),
        grid=(t // tm,),
        in_specs=specs,
        out_specs=tok(d),
        out_shape=jax.ShapeDtypeStruct((t, d), F32),
        compiler_params=_params(1),
        name="ffn_mix_out" if mix is not None else "ffn",
    )(*ins)


def _mixer_in_kernel(h_ref, cos_ref, sin_ref, *refs, q_scale):
    consts, (q_out, k_out, v_out, ylru_out, xbuf, hbuf, hcar) = refs[:-7], refs[-7:]

    @pl.when(pl.program_id(1) == 0)
    def _():
        xbuf[:, 0:SUBLANES, :] = jnp.zeros((xbuf.shape[0], SUBLANES, LANES), F32)
        hcar[...] = jnp.zeros_like(hcar)

    tm = h_ref.shape[0]
    gm = tm // MIXER_ROW_GROUPS
    for r0 in range(0, tm, gm):
        rows = slice(r0, r0 + gm)
        _mixer_group(h_ref.at[rows, :], cos_ref.at[:, rows], sin_ref.at[:, rows], *consts,
                     q_out.at[:, rows], k_out.at[rows, :], v_out.at[0, :, rows], ylru_out.at[rows, :],
                     xbuf, hbuf, hcar, q_scale=q_scale)


def _mixer_group(h_ref, cos_ref, sin_ref, gpre_ref, win_ref, convw_ref, convb_ref, wgate_ref,
                 ba_ref, bx_ref, lam_ref, qg_ref, wqa_ref, kvg_ref, wk_ref, wv_ref,
                 q_out, k_out, v_out, ylru_out, xbuf, hbuf, hcar, *, q_scale):
    tm = h_ref.shape[0]
    lru_w = ylru_out.shape[1]
    q_rank = qg_ref.shape[1]
    kv_rank = kvg_ref.shape[1]
    heads = q_out.shape[0] // HEAD_LANES
    nt_dims = (((1,), (1,)), ((), ()))

    n = _rms(h_ref[...], gpre_ref[...]).astype(BF16)
    proj = jnp.dot(n, win_ref[...], preferred_element_type=F32)
    c0 = 2 * lru_w
    c1 = c0 + q_rank
    c2 = c1 + kv_rank
    xl = proj[:, :lru_w]
    gate = proj[:, lru_w:c0]
    kpe = proj[:, c2:c2 + HEAD_LANES]
    cos = cos_ref[...]
    sin = sin_ref[...]
    half_r = cos.shape[0]
    r_lo, r_mid, r_hi = QK_NOPE_DIM, QK_NOPE_DIM + half_r, QK_NOPE_DIM + 2 * half_r

    def rope_rows(slab):
        x1, x2 = slab[r_lo:r_mid], slab[r_mid:r_hi]
        return [slab[:r_lo], x1 * cos - x2 * sin, x2 * cos + x1 * sin, slab[r_hi:]]

    qn = _rms(proj[:, c0:c1], qg_ref[...]).astype(BF16)
    qa = lax.dot_general(wqa_ref[...], qn, nt_dims, preferred_element_type=F32)
    q_rows = []
    for hd in range(heads):
        q_rows += rope_rows(qa[hd * HEAD_LANES:(hd + 1) * HEAD_LANES])
    q_out[...] = (jnp.concatenate(q_rows, axis=0) * q_scale).astype(BF16)

    kvn = _rms(proj[:, c1:c2], kvg_ref[...]).astype(BF16)
    k_rope = jnp.concatenate(rope_rows(kpe.T), axis=0).T
    k_nope = jnp.dot(kvn, wk_ref[...], preferred_element_type=F32)
    k_out[...] = (k_nope + jnp.tile(k_rope, (1, heads))).astype(BF16)
    v_out[...] = lax.dot_general(wv_ref[...], kvn, nt_dims, preferred_element_type=F32).astype(BF16)

    slabs = lru_w // LANES
    seg_len = tm // SUBLANES
    pitch = seg_len + SUBLANES
    cw = convw_ref[...]
    cb = convb_ref[...]
    taps = cw.shape[0]
    xc_slabs = []
    for c in range(slabs):
        lanes = slice(c * LANES, (c + 1) * LANES)
        for s in range(SUBLANES):
            lo = SUBLANES + s * pitch
            xbuf[c, lo:lo + seg_len, :] = xl[s * seg_len:(s + 1) * seg_len, lanes]
            if s + 1 < SUBLANES:
                xbuf[c, lo + seg_len:lo + pitch, :] = xl[(s + 1) * seg_len - SUBLANES:(s + 1) * seg_len, lanes]
        x = [xbuf[c, pl.ds(SUBLANES + t, SUBLANES, stride=pitch), :] for t in range(1 - taps, seg_len)]
        xbuf[c, 0:SUBLANES, :] = xl[tm - SUBLANES:, lanes]
        w = [cw[k:k + 1, lanes] for k in range(taps)]
        steps = []
        for t in range(seg_len):
            acc = cb[:, lanes]
            for k in range(taps):
                acc = acc + x[t + k] * w[k]
            steps.append(acc)
        xc_slabs.append(jnp.concatenate(steps, axis=0))
    xc = jnp.concatenate(xc_slabs, axis=1)

    xcb = xc.astype(BF16)
    half = lru_w // 2
    g0 = jnp.dot(xcb[:, :half], wgate_ref[0], preferred_element_type=F32)
    g1 = jnp.dot(xcb[:, half:], wgate_ref[1], preferred_element_type=F32)
    r = jax.nn.sigmoid(jnp.concatenate([g0[:, :half], g1[:, :half]], axis=1) + ba_ref[...])
    i = jax.nn.sigmoid(jnp.concatenate([g0[:, half:], g1[:, half:]], axis=1) + bx_ref[...])
    log_a = (-LRU_C * r) * jax.nn.softplus(-lam_ref[...])
    a = jnp.exp(log_a)
    th = jnp.tanh(log_a)
    sq = -2.0 * th / (1.0 - th)
    u = jnp.where(sq > 0.0, sq * lax.rsqrt(sq), 0.0) * (i * xc)

    step = lambda t, v: v[t * SUBLANES:(t + 1) * SUBLANES]
    end, decay = step(0, u), step(0, a)
    for t in range(1, seg_len):
        end = step(t, a) * end + step(t, u)
        decay = step(t, a) * decay
    seg = lax.broadcasted_iota(jnp.int32, end.shape, 0)
    shift = 1
    while shift < SUBLANES:
        keep = seg >= shift
        end = jnp.where(keep, decay * pltpu.roll(end, shift, 0) + end, end)
        decay = jnp.where(keep, decay * pltpu.roll(decay, shift, 0), decay)
        shift *= 2
    h_in = hcar[...]
    state = jnp.where(seg == 0, h_in, pltpu.roll(end + decay * h_in, 1, 0))
    for t in range(seg_len):
        state = step(t, a) * state + step(t, u)
        for c in range(slabs):
            hbuf[c, pl.ds(t, SUBLANES, stride=pitch), :] = state[:, c * LANES:(c + 1) * LANES]
    hcar[...] = jnp.broadcast_to(state[SUBLANES - 1:SUBLANES, :], state.shape)
    h_lru = jnp.concatenate(
        [jnp.concatenate([hbuf[c, s * pitch:s * pitch + seg_len, :] for s in range(SUBLANES)], axis=0)
         for c in range(slabs)], axis=1)
    ylru_out[...] = (h_lru * jax.nn.gelu(gate)).astype(BF16)


def _mixer_in(h, cos_t, sin_t, batch, gpre, win, convw, convb, wgate, ba, bx, lam, qg, wqa, kvg, wk, wv):
    t, d = h.shape
    seq = t // batch
    tm = min(TOKEN_TILE, seq)
    gm = tm // MIXER_ROW_GROUPS
    ns = seq // tm
    lru_w = convw.shape[1]
    tok = lambda w: pl.BlockSpec((tm, w), lambda b, s: (b * ns + s, 0))
    cols = lambda r: pl.BlockSpec((r, tm), lambda b, s: (0, b * ns + s))
    consts = (gpre, win, convw, convb, wgate, ba, bx, lam, qg, wqa, kvg, wk, wv)
    q_scale = float((QK_NOPE_DIM + QK_ROPE_DIM) ** -0.5 * LOG2_E)
    return pl.pallas_call(
        functools.partial(_mixer_in_kernel, q_scale=q_scale),
        grid=(batch, ns),
        in_specs=[tok(d), cols(cos_t.shape[0]), cols(sin_t.shape[0])] + [_const_spec(a.shape) for a in consts],
        out_specs=[cols(wqa.shape[0]), tok(wk.shape[1]),
                   pl.BlockSpec((1, wv.shape[0], tm), lambda b, s: (b * ns + s, 0, 0)), tok(lru_w)],
        out_shape=[jax.ShapeDtypeStruct((wqa.shape[0], t), BF16),
                   jax.ShapeDtypeStruct((t, wk.shape[1]), BF16),
                   jax.ShapeDtypeStruct((t // tm, wv.shape[0], tm), BF16),
                   jax.ShapeDtypeStruct((t, lru_w), BF16)],
        scratch_shapes=[pltpu.VMEM((lru_w // LANES, gm + SUBLANES * SUBLANES + SUBLANES, LANES), F32),
                        pltpu.VMEM((lru_w // LANES, gm + SUBLANES * SUBLANES, LANES), F32),
                        pltpu.VMEM((SUBLANES, lru_w), F32)],
        compiler_params=_params(2),
        name="mixer_in",
    )(h, cos_t, sin_t, *consts)


def _attn_kernel(q_ref, k_ref, vt_ref, o_ref, m_ref, acc_ref, s_ref):
    tile = q_ref.shape[1]
    half = tile // 2
    heads = q_ref.shape[0] // HEAD_LANES
    qi = pl.program_id(1)

    m_ref[...] = jnp.full(m_ref.shape, MASK_VALUE, F32)
    acc_ref[...] = jnp.zeros_like(acc_ref)
    ones_rows = jnp.ones((ONES_ROWS, tile), BF16)
    slab = lambda h: slice(h * HEAD_LANES, (h + 1) * HEAD_LANES)

    def v_rows(kj, h):
        return jnp.concatenate([vt_ref[kj, h * V_HEAD_DIM:(h + 1) * V_HEAD_DIM, :], ones_rows], axis=0)

    def rescale(h, m_cur):
        m_prev = m_ref[h]
        m_new = jnp.maximum(m_prev, m_cur)
        m_ref[h] = m_new
        return m_new, jnp.exp2(m_prev - m_new)

    def full_scores(kj, h, slot):
        kh = k_ref[pl.ds(pl.multiple_of(kj * tile, tile), tile), slab(h)]
        s_ref[slot] = jnp.dot(kh, q_ref[slab(h), :], preferred_element_type=F32)
        return jnp.max(s_ref[slot], axis=0, keepdims=True)

    def full_accumulate(kj, h, slot, m_cur):
        m_new, alpha = rescale(h, m_cur)
        pt = jnp.exp2(s_ref[slot] - m_new).astype(BF16)
        acc_ref[h] = acc_ref[h] * alpha + jnp.dot(v_rows(kj, h), pt, preferred_element_type=F32)

    def diag_scores(kj, h, slot):
        key_chunk = lax.broadcasted_iota(jnp.int32, (half, half), 0) // CHUNK
        query_chunk = lax.broadcasted_iota(jnp.int32, (half, half), 1) // CHUNK
        visible = key_chunk <= query_chunk
        row0 = pl.multiple_of(kj * tile, tile)
        qh = q_ref[slab(h), :]
        early = jnp.dot(k_ref[pl.ds(row0, half), slab(h)], qh, preferred_element_type=F32)
        late = jnp.dot(k_ref[pl.ds(row0 + half, half), slab(h)], qh[:, half:], preferred_element_type=F32)
        s_ref[slot, :half, :half] = jnp.where(visible, early[:, :half], MASK_VALUE)
        s_ref[slot, :half, half:] = early[:, half:]
        s_ref[slot, half:, half:] = jnp.where(visible, late, MASK_VALUE)
        m_left = jnp.max(s_ref[slot, :half, :half], axis=0, keepdims=True)
        m_right = jnp.maximum(jnp.max(s_ref[slot, :half, half:], axis=0, keepdims=True),
                              jnp.max(s_ref[slot, half:, half:], axis=0, keepdims=True))
        return jnp.concatenate([m_left, m_right], axis=1)

    def diag_accumulate(kj, h, slot, m_cur):
        m_new, alpha = rescale(h, m_cur)
        p_early = jnp.exp2(s_ref[slot, :half, :] - m_new).astype(BF16)
        p_late = jnp.exp2(s_ref[slot, half:, half:] - m_new[:, half:]).astype(BF16)
        vt = v_rows(kj, h)
        pv = jnp.dot(vt[:, :half], p_early, preferred_element_type=F32)
        pv_late = jnp.dot(vt[:, half:], p_late, preferred_element_type=F32)
        acc = acc_ref[h] * alpha + pv
        acc_ref[h] = jnp.concatenate([acc[:, :half], acc[:, half:] + pv_late], axis=1)

    def run(tiles):
        units = [(kj, h, diag) for kj, diag in tiles for h in range(heads)]

        def issue(n):
            kj, h, diag = units[n]
            return (diag_scores if diag else full_scores)(kj, h, n % heads)

        pending = [issue(n) for n in range(min(SCORES_AHEAD, len(units)))]
        for n, (kj, h, diag) in enumerate(units):
            if n + SCORES_AHEAD < len(units):
                pending.append(issue(n + SCORES_AHEAD))
            (diag_accumulate if diag else full_accumulate)(kj, h, n % heads, pending.pop(0))

    def tile_pair(j, carry):
        run([(2 * j, False), (2 * j + 1, False)])
        return carry

    lax.fori_loop(0, qi // 2, tile_pair, 0)

    @pl.when(qi % 2 == 1)
    def _():
        run([(qi - 1, False)])

    run([(qi, True)])

    outs = []
    for h in range(heads):
        a = acc_ref[h]
        outs.append(a[:V_HEAD_DIM] / a[V_HEAD_DIM:V_HEAD_DIM + 1])
    o_ref[...] = jnp.concatenate(outs, axis=0).T.astype(o_ref.dtype)


def _attention(qt, k, vt, batch):
    t = k.shape[0]
    seq = t // batch
    tile = vt.shape[2]
    assert tile % (2 * CHUNK) == 0 and seq % tile == 0
    nt = seq // tile
    heads = k.shape[1] // HEAD_LANES
    d_v = vt.shape[1]
    return pl.pallas_call(
        _attn_kernel,
        grid=(batch, nt),
        in_specs=[pl.BlockSpec((qt.shape[0], tile), lambda b, i: (0, b * nt + i)),
                  pl.BlockSpec((seq, k.shape[1]), lambda b, i: (b, 0)),
                  pl.BlockSpec((nt, d_v, tile), lambda b, i: (b, 0, 0))],
        out_specs=pl.BlockSpec((tile, d_v), lambda b, i: (b * nt + i, 0)),
        scratch_shapes=[pltpu.VMEM((heads, 1, tile), F32),
                        pltpu.VMEM((heads, V_HEAD_DIM + ONES_ROWS, tile), F32),
                        pltpu.VMEM((heads, tile, tile), F32)],
        out_shape=jax.ShapeDtypeStruct((t, d_v), BF16),
        compiler_params=_params(2),
        name="attention",
    )(qt, k, vt)


def _head_slab(nope, rope):
    rows, heads = rope.shape[0], rope.shape[1]
    if nope is None:
        nope = jnp.zeros((rows, heads, QK_NOPE_DIM), rope.dtype)
    pad = jnp.zeros((rows, heads, HEAD_LANES - QK_NOPE_DIM - QK_ROPE_DIM), rope.dtype)
    return jnp.concatenate([nope, rope, pad], axis=-1).reshape(rows, heads * HEAD_LANES)


def _block_diag_pairs(w_a, w_x):
    blocks, bd, _ = w_a.shape
    eye = jnp.eye(blocks, dtype=w_a.dtype)
    width = blocks * bd
    half = width // 2
    dense = lambda w: jnp.einsum('nde,nm->ndme', w, eye).reshape(width, width)
    da, dx = dense(w_a), dense(w_x)
    tiles = [jnp.concatenate([da[lo:lo + half, lo:lo + half], dx[lo:lo + half, lo:lo + half]], axis=1)
             for lo in (0, half)]
    return jnp.stack(tiles)


def kernel(x, positions, g_ffn1_pre, g_ffn1_post, w_ffn1_gate, w_ffn1_up, w_ffn1_down, g_mix_pre, g_mix_post, w_in, conv_w, conv_b, w_lru_a, b_lru_a, w_lru_x, b_lru_x, lru_lambda, q_a_norm, w_q_b, kv_a_norm, w_kv_b, w_out, g_ffn2_pre, g_ffn2_post, w_ffn2_gate, w_ffn2_up, w_ffn2_down):
    batch, seq, d = x.shape
    depth = g_ffn1_pre.shape[0]
    lru_w = conv_w.shape[-1]
    q_rank = q_a_norm.shape[-1]
    kv_rank = kv_a_norm.shape[-1]
    row = lambda v: v.reshape(1, -1)

    cos_t, sin_t = _rope_tables(positions)
    h = x.reshape(batch * seq, d)
    for l in range(depth):
        h = _ffn(h, row(g_ffn1_pre[l]), w_ffn1_gate[l].astype(BF16), w_ffn1_up[l].astype(BF16),
                 w_ffn1_down[l].astype(BF16), row(g_ffn1_post[l]))

        wi = w_in[l]
        c_rope = 2 * lru_w + q_rank + kv_rank
        k_rope_w = wi[:, c_rope:].reshape(d, 1, QK_ROPE_DIM)
        win = jnp.concatenate([wi[:, :c_rope], _head_slab(None, k_rope_w)], axis=1).astype(BF16)
        wq = w_q_b[l].reshape(q_rank, MLA_HEADS, QK_NOPE_DIM + QK_ROPE_DIM)
        wqa = _head_slab(wq[..., :QK_NOPE_DIM], wq[..., QK_NOPE_DIM:]).T.astype(BF16)
        wkv = w_kv_b[l].reshape(kv_rank, MLA_HEADS, QK_NOPE_DIM + V_HEAD_DIM)
        wk = _head_slab(wkv[..., :QK_NOPE_DIM], jnp.zeros((kv_rank, MLA_HEADS, QK_ROPE_DIM), F32)).astype(BF16)
        wv = wkv[..., QK_NOPE_DIM:].reshape(kv_rank, MLA_HEADS * V_HEAD_DIM).T.astype(BF16)
        wgate = _block_diag_pairs(w_lru_a[l], w_lru_x[l]).astype(BF16)

        q, k, v, y_lru = _mixer_in(h, cos_t, sin_t, batch, row(g_mix_pre[l]), win, conv_w[l], row(conv_b[l]),
                                   wgate, row(b_lru_a[l]), row(b_lru_x[l]), row(lru_lambda[l]),
                                   row(q_a_norm[l]), wqa, row(kv_a_norm[l]), wk, wv)
        y_mla = _attention(q, k, v, batch)
        h = _ffn(h, row(g_ffn2_pre[l]), w_ffn2_gate[l].astype(BF16), w_ffn2_up[l].astype(BF16),
                 w_ffn2_down[l].astype(BF16), row(g_ffn2_post[l]),
                 mix=(y_lru, y_mla, w_out[l].astype(BF16), row(g_mix_post[l])))
    return h.reshape(batch, seq, d)
```

```python
import functools
import math

import jax
import jax.numpy as jnp
from jax import lax
from jax.experimental import pallas as pl
from jax.experimental.pallas import tpu as pltpu

F32 = jnp.float32
BF16 = jnp.bfloat16

EPS = 1e-6
CHUNK = 64
LRU_C = 8.0
ROPE_THETA = 10000.0
MLA_HEADS = 8
QK_NOPE_DIM = 64
QK_ROPE_DIM = 32
V_HEAD_DIM = 64
HEAD_LANES = 128
MASK_VALUE = -1e30

VMEM_LIMIT_BYTES = 56 * 1024 * 1024
TOKEN_TILE = 512
FFN_TOKEN_TILE = 512
FFN_ROW_GROUPS = 2
MIXER_ROW_GROUPS = 1
SUBLANES = 8
LANES = 128
ONES_ROWS = 16
ROPE_TABLE_TILE = 2048
SCORES_AHEAD = 2
LOG2_E = 1.4426950408889634


def _rms(x, g):
    return x * lax.rsqrt(jnp.mean(x * x, axis=-1, keepdims=True) + EPS) * g


def _params(grid_rank, flags=None):
    return pltpu.CompilerParams(
        dimension_semantics=("arbitrary",) * grid_rank, vmem_limit_bytes=VMEM_LIMIT_BYTES, flags=flags)


def _const_spec(shape):
    nd = len(shape)
    return pl.BlockSpec(shape, lambda *_: (0,) * nd, pipeline_mode=pl.Buffered(1))


def _rope_table_kernel(pos_ref, invf_ref, cos_ref, sin_ref):
    ang = invf_ref[...] * pos_ref[...].astype(F32)
    cos_ref[...] = jnp.cos(ang)
    sin_ref[...] = jnp.sin(ang)


def _rope_tables(positions):
    half = QK_ROPE_DIM // 2
    t = positions.size
    inv_freq = 1.0 / (ROPE_THETA ** (jnp.arange(0, QK_ROPE_DIM, 2, dtype=F32) / QK_ROPE_DIM))
    tn = math.gcd(t, ROPE_TABLE_TILE)
    return pl.pallas_call(
        _rope_table_kernel,
        grid=(t // tn,),
        in_specs=[pl.BlockSpec((1, tn), lambda i: (0, i)), pl.BlockSpec((half, 1), lambda i: (0, 0))],
        out_specs=[pl.BlockSpec((half, tn), lambda i: (0, i))] * 2,
        out_shape=[jax.ShapeDtypeStruct((half, t), F32)] * 2,
        name="rope_tables",
    )(positions.reshape(1, t), inv_freq.reshape(half, 1))


def _ffn_kernel(*refs, with_mix_out, ff_chunks):
    if with_mix_out:
        (h_ref, ylru_ref, ymla_ref, wout_ref, gmix_ref,
         gpre_ref, wg_ref, wu_ref, wd_ref, gpost_ref, o_ref) = refs
    else:
        h_ref, gpre_ref, wg_ref, wu_ref, wd_ref, gpost_ref, o_ref = refs
    tm = h_ref.shape[0]
    groups = [slice(r0, r0 + tm // FFN_ROW_GROUPS) for r0 in range(0, tm, tm // FFN_ROW_GROUPS)]
    hs = []
    for rows in groups:
        h = h_ref[rows, :]
        if with_mix_out:
            lru_w = ylru_ref.shape[1]
            m = jnp.dot(ylru_ref[rows, :], wout_ref[:lru_w, :], preferred_element_type=F32)
            m = m + jnp.dot(ymla_ref[rows, :], wout_ref[lru_w:, :], preferred_element_type=F32)
            h = h + _rms(m, gmix_ref[...])
        hs.append(h)
    for rows, h in zip(groups, hs):
        n = _rms(h, gpre_ref[...]).astype(BF16)
        f = None
        for lo, hi in ff_chunks:
            g = jnp.dot(n, wg_ref[:, lo:hi], preferred_element_type=F32)
            u = jnp.dot(n, wu_ref[:, lo:hi], preferred_element_type=F32)
            a = (g * jax.nn.sigmoid(g) * u).astype(BF16)
            part = jnp.dot(a, wd_ref[lo:hi, :], preferred_element_type=F32)
            f = part if f is None else f + part
        o_ref[rows, :] = h + 0.5 * _rms(f, gpost_ref[...])


def _ffn_chunks(d_ff, target=1536):
    chunks, lo = [], 0
    while lo < d_ff:
        hi = min(lo + target, d_ff)
        chunks.append((lo, hi))
        lo = hi
    return tuple(chunks)


def _ffn(h, gpre, wg, wu, wd, gpost, mix=None):
    t, d = h.shape
    d_ff = wg.shape[1]
    tm = math.gcd(FFN_TOKEN_TILE, t)
    tok = lambda w: pl.BlockSpec((tm, w), lambda i: (i, 0))
    ins, specs = [h], [tok(d)]
    if mix is not None:
        ylru, ymla, wout, gmix = mix
        ins += [ylru, ymla, wout, gmix]
        specs += [tok(ylru.shape[1]), tok(ymla.shape[1]), _const_spec(wout.shape), _const_spec(gmix.shape)]
    ins += [gpre, wg, wu, wd, gpost]
    specs += [_const_spec(a.shape) for a in (gpre, wg, wu, wd, gpost)]
    return pl.pallas_call(
        functools.partial(_ffn_kernel, with_mix_out=mix is not None, ff_chunks=_ffn_chunks(d_ff)),
        grid=(t // tm,),
        in_specs=specs,
        out_specs=tok(d),
        out_shape=jax.ShapeDtypeStruct((t, d), F32),
        compiler_params=_params(1),
        name="ffn_mix_out" if mix is not None else "ffn",
    )(*ins)


def _mixer_in_kernel(h_ref, cos_ref, sin_ref, *refs, q_scale):
    consts, (q_out, k_out, v_out, ylru_out, xbuf, hbuf, hcar) = refs[:-7], refs[-7:]

    @pl.when(pl.program_id(1) == 0)
    def _():
        xbuf[:, 0:SUBLANES, :] = jnp.zeros((xbuf.shape[0], SUBLANES, LANES), F32)
        hcar[...] = jnp.zeros_like(hcar)

    tm = h_ref.shape[0]
    gm = tm // MIXER_ROW_GROUPS
    for r0 in range(0, tm, gm):
        rows = slice(r0, r0 + gm)
        _mixer_group(h_ref.at[rows, :], cos_ref.at[:, rows], sin_ref.at[:, rows], *consts,
                     q_out.at[:, rows], k_out.at[rows, :], v_out.at[0, :, rows], ylru_out.at[rows, :],
                     xbuf, hbuf, hcar, q_scale=q_scale)


def _mixer_group(h_ref, cos_ref, sin_ref, gpre_ref, win_ref, convw_ref, convb_ref, wgate_ref,
                 ba_ref, bx_ref, lam_ref, qg_ref, wqa_ref, kvg_ref, wk_ref, wv_ref,
                 q_out, k_out, v_out, ylru_out, xbuf, hbuf, hcar, *, q_scale):
    tm = h_ref.shape[0]
    lru_w = ylru_out.shape[1]
    q_rank = qg_ref.shape[1]
    kv_rank = kvg_ref.shape[1]
    heads = q_out.shape[0] // HEAD_LANES
    nt_dims = (((1,), (1,)), ((), ()))

    n = _rms(h_ref[...], gpre_ref[...]).astype(BF16)
    proj = jnp.dot(n, win_ref[...], preferred_element_type=F32)
    c0 = 2 * lru_w
    c1 = c0 + q_rank
    c2 = c1 + kv_rank
    xl = proj[:, :lru_w]
    gate = proj[:, lru_w:c0]
    kpe = proj[:, c2:c2 + HEAD_LANES]
    cos = cos_ref[...]
    sin = sin_ref[...]
    half_r = cos.shape[0]
    r_lo, r_mid, r_hi = QK_NOPE_DIM, QK_NOPE_DIM + half_r, QK_NOPE_DIM + 2 * half_r

    def rope_rows(slab):
        x1, x2 = slab[r_lo:r_mid], slab[r_mid:r_hi]
        return [slab[:r_lo], x1 * cos - x2 * sin, x2 * cos + x1 * sin, slab[r_hi:]]

    slabs = lru_w // LANES
    seg_len = tm // SUBLANES
    pitch = seg_len + SUBLANES
    cw = convw_ref[...]
    cb = convb_ref[...]
    taps = cw.shape[0]
    xc_slabs = []
    for c in range(slabs):
        lanes = slice(c * LANES, (c + 1) * LANES)
        for s in range(SUBLANES):
            lo = SUBLANES + s * pitch
            xbuf[c, lo:lo + seg_len, :] = xl[s * seg_len:(s + 1) * seg_len, lanes]
            if s + 1 < SUBLANES:
                xbuf[c, lo + seg_len:lo + pitch, :] = xl[(s + 1) * seg_len - SUBLANES:(s + 1) * seg_len, lanes]
        x = [xbuf[c, pl.ds(SUBLANES + t, SUBLANES, stride=pitch), :] for t in range(1 - taps, seg_len)]
        xbuf[c, 0:SUBLANES, :] = xl[tm - SUBLANES:, lanes]
        w = [cw[k:k + 1, lanes] for k in range(taps)]
        steps = []
        for t in range(seg_len):
            acc = cb[:, lanes]
            for k in range(taps):
                acc = acc + x[t + k] * w[k]
            steps.append(acc)
        xc_slabs.append(jnp.concatenate(steps, axis=0))
    xc = jnp.concatenate(xc_slabs, axis=1)

    xcb = xc.astype(BF16)
    half = lru_w // 2
    g0 = jnp.dot(xcb[:, :half], wgate_ref[0], preferred_element_type=F32)
    g1 = jnp.dot(xcb[:, half:], wgate_ref[1], preferred_element_type=F32)

    qn = _rms(proj[:, c0:c1], qg_ref[...]).astype(BF16)
    qa = lax.dot_general(wqa_ref[...], qn, nt_dims, preferred_element_type=F32)
    q_rows = []
    for hd in range(heads):
        q_rows += rope_rows(qa[hd * HEAD_LANES:(hd + 1) * HEAD_LANES])
    q_out[...] = (jnp.concatenate(q_rows, axis=0) * q_scale).astype(BF16)

    kvn = _rms(proj[:, c1:c2], kvg_ref[...]).astype(BF16)
    k_rope = jnp.concatenate(rope_rows(kpe.T), axis=0).T
    k_nope = jnp.dot(kvn, wk_ref[...], preferred_element_type=F32)
    k_out[...] = (k_nope + jnp.tile(k_rope, (1, heads))).astype(BF16)
    v_out[...] = lax.dot_general(wv_ref[...], kvn, nt_dims, preferred_element_type=F32).astype(BF16)

    r = jax.nn.sigmoid(jnp.concatenate([g0[:, :half], g1[:, :half]], axis=1) + ba_ref[...])
    i = jax.nn.sigmoid(jnp.concatenate([g0[:, half:], g1[:, half:]], axis=1) + bx_ref[...])
    log_a = (-LRU_C * r) * jax.nn.softplus(-lam_ref[...])
    a = jnp.exp(log_a)
    th = jnp.tanh(log_a)
    sq = -2.0 * th / (1.0 - th)
    u = jnp.where(sq > 0.0, sq * lax.rsqrt(sq), 0.0) * (i * xc)

    step = lambda t, v: v[t * SUBLANES:(t + 1) * SUBLANES]
    end, decay = step(0, u), step(0, a)
    for t in range(1, seg_len):
        end = step(t, a) * end + step(t, u)
        decay = step(t, a) * decay
    seg = lax.broadcasted_iota(jnp.int32, end.shape, 0)
    shift = 1
    while shift < SUBLANES:
        keep = seg >= shift
        end = jnp.where(keep, decay * pltpu.roll(end, shift, 0) + end, end)
        decay = jnp.where(keep, decay * pltpu.roll(decay, shift, 0), decay)
        shift *= 2
    h_in = hcar[...]
    state = jnp.where(seg == 0, h_in, pltpu.roll(end + decay * h_in, 1, 0))
    for t in range(seg_len):
        state = step(t, a) * state + step(t, u)
        for c in range(slabs):
            hbuf[c, pl.ds(t, SUBLANES, stride=pitch), :] = state[:, c * LANES:(c + 1) * LANES]
    hcar[...] = jnp.broadcast_to(state[SUBLANES - 1:SUBLANES, :], state.shape)
    h_lru = jnp.concatenate(
        [jnp.concatenate([hbuf[c, s * pitch:s * pitch + seg_len, :] for s in range(SUBLANES)], axis=0)
         for c in range(slabs)], axis=1)
    ylru_out[...] = (h_lru * jax.nn.gelu(gate)).astype(BF16)


def _mixer_in(h, cos_t, sin_t, batch, gpre, win, convw, convb, wgate, ba, bx, lam, qg, wqa, kvg, wk, wv):
    t, d = h.shape
    seq = t // batch
    tm = min(TOKEN_TILE, seq)
    gm = tm // MIXER_ROW_GROUPS
    ns = seq // tm
    lru_w = convw.shape[1]
    tok = lambda w: pl.BlockSpec((tm, w), lambda b, s: (b * ns + s, 0))
    cols = lambda r: pl.BlockSpec((r, tm), lambda b, s: (0, b * ns + s))
    consts = (gpre, win, convw, convb, wgate, ba, bx, lam, qg, wqa, kvg, wk, wv)
    q_scale = float((QK_NOPE_DIM + QK_ROPE_DIM) ** -0.5 * LOG2_E)
    return pl.pallas_call(
        functools.partial(_mixer_in_kernel, q_scale=q_scale),
        grid=(batch, ns),
        in_specs=[tok(d), cols(cos_t.shape[0]), cols(sin_t.shape[0])] + [_const_spec(a.shape) for a in consts],
        out_specs=[cols(wqa.shape[0]), tok(wk.shape[1]),
                   pl.BlockSpec((1, wv.shape[0], tm), lambda b, s: (b * ns + s, 0, 0)), tok(lru_w)],
        out_shape=[jax.ShapeDtypeStruct((wqa.shape[0], t), BF16),
                   jax.ShapeDtypeStruct((t, wk.shape[1]), BF16),
                   jax.ShapeDtypeStruct((t // tm, wv.shape[0], tm), BF16),
                   jax.ShapeDtypeStruct((t, lru_w), BF16)],
        scratch_shapes=[pltpu.VMEM((lru_w // LANES, gm + SUBLANES * SUBLANES + SUBLANES, LANES), F32),
                        pltpu.VMEM((lru_w // LANES, gm + SUBLANES * SUBLANES, LANES), F32),
                        pltpu.VMEM((SUBLANES, lru_w), F32)],
        compiler_params=_params(2),
        name="mixer_in",
    )(h, cos_t, sin_t, *consts)


def _attn_kernel(q_ref, k_ref, vt_ref, *refs, n_cast):
    cast_in, o_ref, cast_out = refs[:n_cast], refs[n_cast], refs[n_cast + 1:2 * n_cast + 1]
    m_ref, acc_ref, s_ref = refs[2 * n_cast + 1:]
    for src, dst in zip(cast_in, cast_out):
        dst[...] = src[...].astype(dst.dtype)

    tile = q_ref.shape[1]
    half = tile // 2
    heads = q_ref.shape[0] // HEAD_LANES
    qi = pl.program_id(1)

    m_ref[...] = jnp.full(m_ref.shape, MASK_VALUE, F32)
    acc_ref[...] = jnp.zeros_like(acc_ref)
    ones_rows = jnp.ones((ONES_ROWS, tile), BF16)
    slab = lambda h: slice(h * HEAD_LANES, (h + 1) * HEAD_LANES)

    def v_rows(kj, h):
        return jnp.concatenate([vt_ref[kj, h * V_HEAD_DIM:(h + 1) * V_HEAD_DIM, :], ones_rows], axis=0)

    def rescale(h, m_cur):
        m_prev = m_ref[h]
        m_new = jnp.maximum(m_prev, m_cur)
        m_ref[h] = m_new
        return m_new, jnp.exp2(m_prev - m_new)

    def full_scores(kj, h, slot):
        kh = k_ref[pl.ds(pl.multiple_of(kj * tile, tile), tile), slab(h)]
        s_ref[slot] = jnp.dot(kh, q_ref[slab(h), :], preferred_element_type=F32)
        return jnp.max(s_ref[slot], axis=0, keepdims=True)

    def full_accumulate(kj, h, slot, m_cur):
        m_new, alpha = rescale(h, m_cur)
        pt = jnp.exp2(s_ref[slot] - m_new).astype(BF16)
        acc_ref[h] = acc_ref[h] * alpha + jnp.dot(v_rows(kj, h), pt, preferred_element_type=F32)

    def diag_scores(kj, h, slot):
        key_chunk = lax.broadcasted_iota(jnp.int32, (half, half), 0) // CHUNK
        query_chunk = lax.broadcasted_iota(jnp.int32, (half, half), 1) // CHUNK
        visible = key_chunk <= query_chunk
        row0 = pl.multiple_of(kj * tile, tile)
        qh = q_ref[slab(h), :]
        early = jnp.dot(k_ref[pl.ds(row0, half), slab(h)], qh, preferred_element_type=F32)
        late = jnp.dot(k_ref[pl.ds(row0 + half, half), slab(h)], qh[:, half:], preferred_element_type=F32)
        s_ref[slot, :half, :half] = jnp.where(visible, early[:, :half], MASK_VALUE)
        s_ref[slot, :half, half:] = early[:, half:]
        s_ref[slot, half:, half:] = jnp.where(visible, late, MASK_VALUE)
        m_left = jnp.max(s_ref[slot, :half, :half], axis=0, keepdims=True)
        m_right = jnp.maximum(jnp.max(s_ref[slot, :half, half:], axis=0, keepdims=True),
                              jnp.max(s_ref[slot, half:, half:], axis=0, keepdims=True))
        return jnp.concatenate([m_left, m_right], axis=1)

    def diag_accumulate(kj, h, slot, m_cur):
        m_new, alpha = rescale(h, m_cur)
        p_early = jnp.exp2(s_ref[slot, :half, :] - m_new).astype(BF16)
        p_late = jnp.exp2(s_ref[slot, half:, half:] - m_new[:, half:]).astype(BF16)
        vt = v_rows(kj, h)
        pv = jnp.dot(vt[:, :half], p_early, preferred_element_type=F32)
        pv_late = jnp.dot(vt[:, half:], p_late, preferred_element_type=F32)
        acc = acc_ref[h] * alpha + pv
        acc_ref[h] = jnp.concatenate([acc[:, :half], acc[:, half:] + pv_late], axis=1)

    def run(tiles):
        units = [(kj, h, diag) for kj, diag in tiles for h in range(heads)]

        def issue(n):
            kj, h, diag = units[n]
            return (diag_scores if diag else full_scores)(kj, h, n % heads)

        pending = [issue(n) for n in range(min(SCORES_AHEAD, len(units)))]
        for n, (kj, h, diag) in enumerate(units):
            if n + SCORES_AHEAD < len(units):
                pending.append(issue(n + SCORES_AHEAD))
            (diag_accumulate if diag else full_accumulate)(kj, h, n % heads, pending.pop(0))

    def tile_pair(j, carry):
        run([(2 * j, False), (2 * j + 1, False)])
        return carry

    lax.fori_loop(0, qi // 2, tile_pair, 0)

    @pl.when(qi % 2 == 1)
    def _():
        run([(qi - 1, False)])

    run([(qi, True)])

    outs = []
    for h in range(heads):
        a = acc_ref[h]
        outs.append(a[:V_HEAD_DIM] / a[V_HEAD_DIM:V_HEAD_DIM + 1])
    o_ref[...] = jnp.concatenate(outs, axis=0).T.astype(o_ref.dtype)


def _attention(qt, k, vt, batch, to_bf16=()):
    t = k.shape[0]
    seq = t // batch
    tile = vt.shape[2]
    assert tile % (2 * CHUNK) == 0 and seq % tile == 0
    nt = seq // tile
    heads = k.shape[1] // HEAD_LANES
    d_v = vt.shape[1]
    steps = batch * nt
    cast_specs = [pl.BlockSpec((w.shape[0] // steps, w.shape[1]), lambda b, i: (b * nt + i, 0)) for w in to_bf16]
    outs = pl.pallas_call(
        functools.partial(_attn_kernel, n_cast=len(to_bf16)),
        grid=(batch, nt),
        in_specs=[pl.BlockSpec((qt.shape[0], tile), lambda b, i: (0, b * nt + i)),
                  pl.BlockSpec((seq, k.shape[1]), lambda b, i: (b, 0)),
                  pl.BlockSpec((nt, d_v, tile), lambda b, i: (b, 0, 0))] + cast_specs,
        out_specs=[pl.BlockSpec((tile, d_v), lambda b, i: (b * nt + i, 0))] + cast_specs,
        scratch_shapes=[pltpu.VMEM((heads, 1, tile), F32),
                        pltpu.VMEM((heads, V_HEAD_DIM + ONES_ROWS, tile), F32),
                        pltpu.VMEM((heads, tile, tile), F32)],
        out_shape=[jax.ShapeDtypeStruct((t, d_v), BF16)] + [jax.ShapeDtypeStruct(w.shape, BF16) for w in to_bf16],
        compiler_params=_params(2),
        name="attention",
    )(qt, k, vt, *to_bf16)
    return outs[0], outs[1:]


def _rides_along(w, steps):
    return w.shape[0] % steps == 0 and (w.shape[0] // steps) % (2 * SUBLANES) == 0


def _head_slab(nope, rope):
    rows, heads = rope.shape[0], rope.shape[1]
    if nope is None:
        nope = jnp.zeros((rows, heads, QK_NOPE_DIM), rope.dtype)
    pad = jnp.zeros((rows, heads, HEAD_LANES - QK_NOPE_DIM - QK_ROPE_DIM), rope.dtype)
    return jnp.concatenate([nope, rope, pad], axis=-1).reshape(rows, heads * HEAD_LANES)


def _block_diag_pairs(w_a, w_x):
    blocks, bd, _ = w_a.shape
    eye = jnp.eye(blocks, dtype=w_a.dtype)
    width = blocks * bd
    half = width // 2
    dense = lambda w: jnp.einsum('nde,nm->ndme', w, eye).reshape(width, width)
    da, dx = dense(w_a), dense(w_x)
    tiles = [jnp.concatenate([da[lo:lo + half, lo:lo + half], dx[lo:lo + half, lo:lo + half]], axis=1)
             for lo in (0, half)]
    return jnp.stack(tiles)


def kernel(x, positions, g_ffn1_pre, g_ffn1_post, w_ffn1_gate, w_ffn1_up, w_ffn1_down, g_mix_pre, g_mix_post, w_in, conv_w, conv_b, w_lru_a, b_lru_a, w_lru_x, b_lru_x, lru_lambda, q_a_norm, w_q_b, kv_a_norm, w_kv_b, w_out, g_ffn2_pre, g_ffn2_post, w_ffn2_gate, w_ffn2_up, w_ffn2_down):
    batch, seq, d = x.shape
    depth = g_ffn1_pre.shape[0]
    lru_w = conv_w.shape[-1]
    q_rank = q_a_norm.shape[-1]
    kv_rank = kv_a_norm.shape[-1]
    row = lambda v: v.reshape(1, -1)

    cos_t, sin_t = _rope_tables(positions)
    h = x.reshape(batch * seq, d)
    for l in range(depth):
        h = _ffn(h, row(g_ffn1_pre[l]), w_ffn1_gate[l].astype(BF16), w_ffn1_up[l].astype(BF16),
                 w_ffn1_down[l].astype(BF16), row(g_ffn1_post[l]))

        wi = w_in[l]
        c_rope = 2 * lru_w + q_rank + kv_rank
        k_rope_w = wi[:, c_rope:].reshape(d, 1, QK_ROPE_DIM)
        win = jnp.concatenate([wi[:, :c_rope], _head_slab(None, k_rope_w)], axis=1).astype(BF16)
        wq = w_q_b[l].reshape(q_rank, MLA_HEADS, QK_NOPE_DIM + QK_ROPE_DIM)
        wqa = _head_slab(wq[..., :QK_NOPE_DIM], wq[..., QK_NOPE_DIM:]).T.astype(BF16)
        wkv = w_kv_b[l].reshape(kv_rank, MLA_HEADS, QK_NOPE_DIM + V_HEAD_DIM)
        wk = _head_slab(wkv[..., :QK_NOPE_DIM], jnp.zeros((kv_rank, MLA_HEADS, QK_ROPE_DIM), F32)).astype(BF16)
        wv = wkv[..., QK_NOPE_DIM:].reshape(kv_rank, MLA_HEADS * V_HEAD_DIM).T.astype(BF16)
        wgate = _block_diag_pairs(w_lru_a[l], w_lru_x[l]).astype(BF16)

        q, k, v, y_lru = _mixer_in(h, cos_t, sin_t, batch, row(g_mix_pre[l]), win, conv_w[l], row(conv_b[l]),
                                   wgate, row(b_lru_a[l]), row(b_lru_x[l]), row(lru_lambda[l]),
                                   row(q_a_norm[l]), wqa, row(kv_a_norm[l]), wk, wv)
        late_w = [w_ffn2_gate[l], w_ffn2_up[l], w_ffn2_down[l].reshape(w_ffn2_gate[l].shape), w_out[l]]
        if all(_rides_along(w, (batch * seq) // v.shape[2]) for w in late_w):
            y_mla, late_w = _attention(q, k, v, batch, to_bf16=late_w)
        else:
            y_mla, _ = _attention(q, k, v, batch)
            late_w = [w.astype(BF16) for w in late_w]
        wg2, wu2, wd2, wo = late_w
        h = _ffn(h, row(g_ffn2_pre[l]), wg2, wu2, wd2.reshape(w_ffn2_down[l].shape), row(g_ffn2_post[l]),
                 mix=(y_lru, y_mla, wo, row(g_mix_post[l])))
    return h.reshape(batch, seq, d)
```

```python
import functools
import math

import jax
import jax.numpy as jnp
from jax import lax
from jax.experimental import pallas as pl
from jax.experimental.pallas import tpu as pltpu

F32 = jnp.float32
BF16 = jnp.bfloat16

EPS = 1e-6
CHUNK = 64
LRU_C = 8.0
ROPE_THETA = 10000.0
MLA_HEADS = 8
QK_NOPE_DIM = 64
QK_ROPE_DIM = 32
V_HEAD_DIM = 64
HEAD_LANES = 128
MASK_VALUE = -1e30

VMEM_LIMIT_BYTES = 56 * 1024 * 1024
TOKEN_TILE = 512
FFN_TOKEN_TILE = 512
FFN_ROW_GROUPS = 2
MIXER_ROW_GROUPS = 1
SUBLANES = 8
LANES = 128
ONES_ROWS = 16
ROPE_TABLE_TILE = 2048
SCORES_AHEAD = 2
LOG2_E = 1.4426950408889634


def _rms(x, g):
    return x * lax.rsqrt(jnp.mean(x * x, axis=-1, keepdims=True) + EPS) * g


def _params(grid_rank, flags=None):
    return pltpu.CompilerParams(
        dimension_semantics=("arbitrary",) * grid_rank, vmem_limit_bytes=VMEM_LIMIT_BYTES, flags=flags)


def _const_spec(shape):
    nd = len(shape)
    return pl.BlockSpec(shape, lambda *_: (0,) * nd, pipeline_mode=pl.Buffered(1))


def _rope_table_kernel(pos_ref, invf_ref, cos_ref, sin_ref):
    ang = invf_ref[...] * pos_ref[...].astype(F32)
    cos_ref[...] = jnp.cos(ang)
    sin_ref[...] = jnp.sin(ang)


def _rope_tables(positions):
    half = QK_ROPE_DIM // 2
    t = positions.size
    inv_freq = 1.0 / (ROPE_THETA ** (jnp.arange(0, QK_ROPE_DIM, 2, dtype=F32) / QK_ROPE_DIM))
    tn = math.gcd(t, ROPE_TABLE_TILE)
    return pl.pallas_call(
        _rope_table_kernel,
        grid=(t // tn,),
        in_specs=[pl.BlockSpec((1, tn), lambda i: (0, i)), pl.BlockSpec((half, 1), lambda i: (0, 0))],
        out_specs=[pl.BlockSpec((half, tn), lambda i: (0, i))] * 2,
        out_shape=[jax.ShapeDtypeStruct((half, t), F32)] * 2,
        name="rope_tables",
    )(positions.reshape(1, t), inv_freq.reshape(half, 1))


def _ffn_kernel(*refs, with_mix_out, ff_chunks):
    if with_mix_out:
        (h_ref, ylru_ref, ymla_ref, wout_ref, gmix_ref,
         gpre_ref, wg_ref, wu_ref, wd_ref, gpost_ref, o_ref) = refs
    else:
        h_ref, gpre_ref, wg_ref, wu_ref, wd_ref, gpost_ref, o_ref = refs
    tm = h_ref.shape[0]
    groups = [slice(r0, r0 + tm // FFN_ROW_GROUPS) for r0 in range(0, tm, tm // FFN_ROW_GROUPS)]
    hs = []
    for rows in groups:
        h = h_ref[rows, :]
        if with_mix_out:
            lru_w = ylru_ref.shape[1]
            m = jnp.dot(ylru_ref[rows, :], wout_ref[:lru_w, :], preferred_element_type=F32)
            m = m + jnp.dot(ymla_ref[rows, :], wout_ref[lru_w:, :], preferred_element_type=F32)
            h = h + _rms(m, gmix_ref[...])
        hs.append(h)
    for rows, h in zip(groups, hs):
        n = _rms(h, gpre_ref[...]).astype(BF16)
        f = None
        for lo, hi in ff_chunks:
            g = jnp.dot(n, wg_ref[:, lo:hi], preferred_element_type=F32)
            u = jnp.dot(n, wu_ref[:, lo:hi], preferred_element_type=F32)
            a = (g * jax.nn.sigmoid(g) * u).astype(BF16)
            part = jnp.dot(a, wd_ref[lo:hi, :], preferred_element_type=F32)
            f = part if f is None else f + part
        o_ref[rows, :] = h + 0.5 * _rms(f, gpost_ref[...])


def _ffn_chunks(d_ff, target=1536):
    chunks, lo = [], 0
    while lo < d_ff:
        hi = min(lo + target, d_ff)
        chunks.append((lo, hi))
        lo = hi
    return tuple(chunks)


def _ffn(h, gpre, wg, wu, wd, gpost, mix=None):
    t, d = h.shape
    d_ff = wg.shape[1]
    tm = math.gcd(FFN_TOKEN_TILE, t)
    tok = lambda w: pl.BlockSpec((tm, w), lambda i: (i, 0))
    ins, specs = [h], [tok(d)]
    if mix is not None:
        ylru, ymla, wout, gmix = mix
        ins += [ylru, ymla, wout, gmix]
        specs += [tok(ylru.shape[1]), tok(ymla.shape[1]), _const_spec(wout.shape), _const_spec(gmix.shape)]
    ins += [gpre, wg, wu, wd, gpost]
    specs += [_const_spec(a.shape) for a in (gpre, wg, wu, wd, gpost)]
    return pl.pallas_call(
        functools.partial(_ffn_kernel, with_mix_out=mix is not None, ff_chunks=_ffn_chunks(d_ff)),
        grid=(t // tm,),
        in_specs=specs,
        out_specs=tok(d),
        out_shape=jax.ShapeDtypeStruct((t, d), F32),
        compiler_params=_params(1),
        name="ffn_mix_out" if mix is not None else "ffn",
    )(*ins)


def _mixer_in_kernel(h_ref, cos_ref, sin_ref, *refs, q_scale):
    consts, (q_out, k_out, v_out, ylru_out, xbuf, hbuf, hcar) = refs[:-7], refs[-7:]

    @pl.when(pl.program_id(1) == 0)
    def _():
        xbuf[:, 0:SUBLANES, :] = jnp.zeros((xbuf.shape[0], SUBLANES, LANES), F32)
        hcar[...] = jnp.zeros_like(hcar)

    tm = h_ref.shape[0]
    gm = tm // MIXER_ROW_GROUPS
    for r0 in range(0, tm, gm):
        rows = slice(r0, r0 + gm)
        _mixer_group(h_ref.at[rows, :], cos_ref.at[:, rows], sin_ref.at[:, rows], *consts,
                     q_out.at[:, rows], k_out.at[rows, :], v_out.at[0, :, rows], ylru_out.at[rows, :],
                     xbuf, hbuf, hcar, q_scale=q_scale)


def _mixer_group(h_ref, cos_ref, sin_ref, gpre_ref, win_ref, convw_ref, convb_ref, wgate_ref,
                 ba_ref, bx_ref, lam_ref, qg_ref, wqa_ref, kvg_ref, wk_ref, wv_ref,
                 q_out, k_out, v_out, ylru_out, xbuf, hbuf, hcar, *, q_scale):
    tm = h_ref.shape[0]
    lru_w = ylru_out.shape[1]
    q_rank = qg_ref.shape[1]
    kv_rank = kvg_ref.shape[1]
    heads = q_out.shape[0] // HEAD_LANES
    nt_dims = (((1,), (1,)), ((), ()))

    n = _rms(h_ref[...], gpre_ref[...]).astype(BF16)
    proj = jnp.dot(n, win_ref[...], preferred_element_type=F32)
    c0 = 2 * lru_w
    c1 = c0 + q_rank
    c2 = c1 + kv_rank
    xl = proj[:, :lru_w]
    gate = proj[:, lru_w:c0]
    kpe = proj[:, c2:c2 + HEAD_LANES]
    cos = cos_ref[...]
    sin = sin_ref[...]
    half_r = cos.shape[0]
    r_lo, r_mid, r_hi = QK_NOPE_DIM, QK_NOPE_DIM + half_r, QK_NOPE_DIM + 2 * half_r

    def rope_rows(slab):
        x1, x2 = slab[r_lo:r_mid], slab[r_mid:r_hi]
        return [slab[:r_lo], x1 * cos - x2 * sin, x2 * cos + x1 * sin, slab[r_hi:]]

    slabs = lru_w // LANES
    seg_len = tm // SUBLANES
    pitch = seg_len + SUBLANES
    cw = convw_ref[...]
    cb = convb_ref[...]
    taps = cw.shape[0]
    xc_slabs = []
    for c in range(slabs):
        lanes = slice(c * LANES, (c + 1) * LANES)
        for s in range(SUBLANES):
            lo = SUBLANES + s * pitch
            xbuf[c, lo:lo + seg_len, :] = xl[s * seg_len:(s + 1) * seg_len, lanes]
            if s + 1 < SUBLANES:
                xbuf[c, lo + seg_len:lo + pitch, :] = xl[(s + 1) * seg_len - SUBLANES:(s + 1) * seg_len, lanes]
        x = [xbuf[c, pl.ds(SUBLANES + t, SUBLANES, stride=pitch), :] for t in range(1 - taps, seg_len)]
        xbuf[c, 0:SUBLANES, :] = xl[tm - SUBLANES:, lanes]
        w = [cw[k:k + 1, lanes] for k in range(taps)]
        steps = []
        for t in range(seg_len):
            acc = cb[:, lanes]
            for k in range(taps):
                acc = acc + x[t + k] * w[k]
            steps.append(acc)
        xc_slabs.append(jnp.concatenate(steps, axis=0))
    xc = jnp.concatenate(xc_slabs, axis=1)

    xcb = xc.astype(BF16)
    half = lru_w // 2
    g0 = jnp.dot(xcb[:, :half], wgate_ref[0], preferred_element_type=F32)
    g1 = jnp.dot(xcb[:, half:], wgate_ref[1], preferred_element_type=F32)

    qn = _rms(proj[:, c0:c1], qg_ref[...]).astype(BF16)
    qa = lax.dot_general(wqa_ref[...], qn, nt_dims, preferred_element_type=F32)
    q_rows = []
    for hd in range(heads):
        q_rows += rope_rows(qa[hd * HEAD_LANES:(hd + 1) * HEAD_LANES])
    q_out[...] = (jnp.concatenate(q_rows, axis=0) * q_scale).astype(BF16)

    kvn = _rms(proj[:, c1:c2], kvg_ref[...]).astype(BF16)
    k_rope = jnp.concatenate(rope_rows(kpe.T), axis=0).T
    k_nope = jnp.dot(kvn, wk_ref[...], preferred_element_type=F32)
    k_out[...] = (k_nope + jnp.tile(k_rope, (1, heads))).astype(BF16)
    v_out[...] = lax.dot_general(wv_ref[...], kvn, nt_dims, preferred_element_type=F32).astype(BF16)

    r = jax.nn.sigmoid(jnp.concatenate([g0[:, :half], g1[:, :half]], axis=1) + ba_ref[...])
    i = jax.nn.sigmoid(jnp.concatenate([g0[:, half:], g1[:, half:]], axis=1) + bx_ref[...])
    log_a = (-LRU_C * r) * jax.nn.softplus(-lam_ref[...])
    a = jnp.exp(log_a)
    th = jnp.tanh(log_a)
    sq = -2.0 * th / (1.0 - th)
    u = jnp.where(sq > 0.0, sq * lax.rsqrt(sq), 0.0) * (i * xc)

    step = lambda t, v: v[t * SUBLANES:(t + 1) * SUBLANES]
    end, decay = step(0, u), step(0, a)
    for t in range(1, seg_len):
        end = step(t, a) * end + step(t, u)
        decay = step(t, a) * decay
    seg = lax.broadcasted_iota(jnp.int32, end.shape, 0)
    shift = 1
    while shift < SUBLANES:
        keep = seg >= shift
        end = jnp.where(keep, decay * pltpu.roll(end, shift, 0) + end, end)
        decay = jnp.where(keep, decay * pltpu.roll(decay, shift, 0), decay)
        shift *= 2
    h_in = hcar[...]
    state = jnp.where(seg == 0, h_in, pltpu.roll(end + decay * h_in, 1, 0))
    for t in range(seg_len):
        state = step(t, a) * state + step(t, u)
        for c in range(slabs):
            hbuf[c, pl.ds(t, SUBLANES, stride=pitch), :] = state[:, c * LANES:(c + 1) * LANES]
    hcar[...] = jnp.broadcast_to(state[SUBLANES - 1:SUBLANES, :], state.shape)
    h_lru = jnp.concatenate(
        [jnp.concatenate([hbuf[c, s * pitch:s * pitch + seg_len, :] for s in range(SUBLANES)], axis=0)
         for c in range(slabs)], axis=1)
    ylru_out[...] = (h_lru * jax.nn.gelu(gate)).astype(BF16)


def _mixer_in(h, cos_t, sin_t, batch, gpre, win, convw, convb, wgate, ba, bx, lam, qg, wqa, kvg, wk, wv):
    t, d = h.shape
    seq = t // batch
    tm = min(TOKEN_TILE, seq)
    gm = tm // MIXER_ROW_GROUPS
    ns = seq // tm
    lru_w = convw.shape[1]
    tok = lambda w: pl.BlockSpec((tm, w), lambda b, s: (b * ns + s, 0))
    cols = lambda r: pl.BlockSpec((r, tm), lambda b, s: (0, b * ns + s))
    consts = (gpre, win, convw, convb, wgate, ba, bx, lam, qg, wqa, kvg, wk, wv)
    q_scale = float((QK_NOPE_DIM + QK_ROPE_DIM) ** -0.5 * LOG2_E)
    return pl.pallas_call(
        functools.partial(_mixer_in_kernel, q_scale=q_scale),
        grid=(batch, ns),
        in_specs=[tok(d), cols(cos_t.shape[0]), cols(sin_t.shape[0])] + [_const_spec(a.shape) for a in consts],
        out_specs=[cols(wqa.shape[0]), tok(wk.shape[1]),
                   pl.BlockSpec((1, wv.shape[0], tm), lambda b, s: (b * ns + s, 0, 0)), tok(lru_w)],
        out_shape=[jax.ShapeDtypeStruct((wqa.shape[0], t), BF16),
                   jax.ShapeDtypeStruct((t, wk.shape[1]), BF16),
                   jax.ShapeDtypeStruct((t // tm, wv.shape[0], tm), BF16),
                   jax.ShapeDtypeStruct((t, lru_w), BF16)],
        scratch_shapes=[pltpu.VMEM((lru_w // LANES, gm + SUBLANES * SUBLANES + SUBLANES, LANES), F32),
                        pltpu.VMEM((lru_w // LANES, gm + SUBLANES * SUBLANES, LANES), F32),
                        pltpu.VMEM((SUBLANES, lru_w), F32)],
        compiler_params=_params(2),
        name="mixer_in",
    )(h, cos_t, sin_t, *consts)


def _attn_kernel(q_ref, k_ref, vt_ref, *refs, n_cast):
    cast_in, o_ref, cast_out = refs[:n_cast], refs[n_cast], refs[n_cast + 1:2 * n_cast + 1]
    m_ref, acc_ref, s_ref = refs[2 * n_cast + 1:]
    for src, dst in zip(cast_in, cast_out):
        dst[...] = src[...].astype(dst.dtype)

    tile = q_ref.shape[1]
    half = tile // 2
    heads = q_ref.shape[0] // HEAD_LANES
    qi = pl.program_id(1)

    m_ref[...] = jnp.full(m_ref.shape, MASK_VALUE, F32)
    acc_ref[...] = jnp.zeros_like(acc_ref)
    ones_rows = jnp.ones((ONES_ROWS, tile), BF16)
    slab = lambda h: slice(h * HEAD_LANES, (h + 1) * HEAD_LANES)

    def v_rows(kj, h):
        return jnp.concatenate([vt_ref[kj, h * V_HEAD_DIM:(h + 1) * V_HEAD_DIM, :], ones_rows], axis=0)

    def rescale(h, m_cur):
        m_prev = m_ref[h]
        m_new = jnp.maximum(m_prev, m_cur)
        m_ref[h] = m_new
        return m_new, jnp.exp2(m_prev - m_new)

    def full_scores(kj, h, slot):
        kh = k_ref[pl.ds(pl.multiple_of(kj * tile, tile), tile), slab(h)]
        s_ref[slot] = jnp.dot(kh, q_ref[slab(h), :], preferred_element_type=F32)
        return jnp.max(s_ref[slot], axis=0, keepdims=True)

    def full_accumulate(kj, h, slot, m_cur):
        m_new, alpha = rescale(h, m_cur)
        pt = jnp.exp2(s_ref[slot] - m_new).astype(BF16)
        acc_ref[h] = acc_ref[h] * alpha + jnp.dot(v_rows(kj, h), pt, preferred_element_type=F32)

    def diag_scores(kj, h, slot):
        key_chunk = lax.broadcasted_iota(jnp.int32, (half, half), 0) // CHUNK
        query_chunk = lax.broadcasted_iota(jnp.int32, (half, half), 1) // CHUNK
        visible = key_chunk <= query_chunk
        row0 = pl.multiple_of(kj * tile, tile)
        qh = q_ref[slab(h), :]
        early = jnp.dot(k_ref[pl.ds(row0, half), slab(h)], qh, preferred_element_type=F32)
        late = jnp.dot(k_ref[pl.ds(row0 + half, half), slab(h)], qh[:, half:], preferred_element_type=F32)
        s_ref[slot, :half, :half] = jnp.where(visible, early[:, :half], MASK_VALUE)
        s_ref[slot, :half, half:] = early[:, half:]
        s_ref[slot, half:, half:] = jnp.where(visible, late, MASK_VALUE)
        m_left = jnp.max(s_ref[slot, :half, :half], axis=0, keepdims=True)
        m_right = jnp.maximum(jnp.max(s_ref[slot, :half, half:], axis=0, keepdims=True),
                              jnp.max(s_ref[slot, half:, half:], axis=0, keepdims=True))
        return jnp.concatenate([m_left, m_right], axis=1)

    def diag_accumulate(kj, h, slot, m_cur):
        m_new, alpha = rescale(h, m_cur)
        p_early = jnp.exp2(s_ref[slot, :half, :] - m_new).astype(BF16)
        p_late = jnp.exp2(s_ref[slot, half:, half:] - m_new[:, half:]).astype(BF16)
        vt = v_rows(kj, h)
        pv = jnp.dot(vt[:, :half], p_early, preferred_element_type=F32)
        pv_late = jnp.dot(vt[:, half:], p_late, preferred_element_type=F32)
        acc = acc_ref[h] * alpha + pv
        acc_ref[h] = jnp.concatenate([acc[:, :half], acc[:, half:] + pv_late], axis=1)

    def run(tiles):
        units = [(kj, h, diag) for kj, diag in tiles for h in range(heads)]

        def issue(n):
            kj, h, diag = units[n]
            return (diag_scores if diag else full_scores)(kj, h, n % heads)

        pending = [issue(n) for n in range(min(SCORES_AHEAD, len(units)))]
        for n, (kj, h, diag) in enumerate(units):
            if n + SCORES_AHEAD < len(units):
                pending.append(issue(n + SCORES_AHEAD))
            (diag_accumulate if diag else full_accumulate)(kj, h, n % heads, pending.pop(0))

    def tile_pair(j, carry):
        run([(2 * j, False), (2 * j + 1, False)])
        return carry

    lax.fori_loop(0, qi // 2, tile_pair, 0)

    @pl.when(qi % 2 == 1)
    def _():
        run([(qi - 1, False)])

    run([(qi, True)])

    outs = []
    for h in range(heads):
        a = acc_ref[h]
        outs.append(a[:V_HEAD_DIM] / a[V_HEAD_DIM:V_HEAD_DIM + 1])
    o_ref[...] = jnp.concatenate(outs, axis=0).T.astype(o_ref.dtype)


def _attention(qt, k, vt, batch, to_bf16=()):
    t = k.shape[0]
    seq = t // batch
    tile = vt.shape[2]
    assert tile % (2 * CHUNK) == 0 and seq % tile == 0
    nt = seq // tile
    heads = k.shape[1] // HEAD_LANES
    d_v = vt.shape[1]
    steps = batch * nt
    cast_in_specs, cast_out_specs = [], []
    for w, layer in to_bf16:
        rows = _cast_rows(w.shape[1], steps)
        last = w.shape[1] // rows - 1
        cast_in_specs.append(pl.BlockSpec(
            (None, rows, w.shape[2]), lambda b, i, layer=layer, last=last: (layer, jnp.minimum(b * nt + i, last), 0)))
        cast_out_specs.append(pl.BlockSpec(
            (rows, w.shape[2]), lambda b, i, last=last: (jnp.minimum(b * nt + i, last), 0)))
    outs = pl.pallas_call(
        functools.partial(_attn_kernel, n_cast=len(to_bf16)),
        grid=(batch, nt),
        in_specs=[pl.BlockSpec((qt.shape[0], tile), lambda b, i: (0, b * nt + i)),
                  pl.BlockSpec((seq, k.shape[1]), lambda b, i: (b, 0)),
                  pl.BlockSpec((nt, d_v, tile), lambda b, i: (b, 0, 0))] + cast_in_specs,
        out_specs=[pl.BlockSpec((tile, d_v), lambda b, i: (b * nt + i, 0))] + cast_out_specs,
        scratch_shapes=[pltpu.VMEM((heads, 1, tile), F32),
                        pltpu.VMEM((heads, V_HEAD_DIM + ONES_ROWS, tile), F32),
                        pltpu.VMEM((heads, tile, tile), F32)],
        out_shape=[jax.ShapeDtypeStruct((t, d_v), BF16)]
                  + [jax.ShapeDtypeStruct(w.shape[1:], BF16) for w, _ in to_bf16],
        compiler_params=_params(2),
        name="attention",
    )(qt, k, vt, *[w for w, _ in to_bf16])
    return outs[0], outs[1:]


def _cast_rows(n_rows, steps):
    for rows in range(2 * SUBLANES, n_rows + 1, 2 * SUBLANES):
        if n_rows % rows == 0 and n_rows // rows <= steps:
            return rows
    return None


def _head_slab(nope, rope):
    rows, heads = rope.shape[0], rope.shape[1]
    if nope is None:
        nope = jnp.zeros((rows, heads, QK_NOPE_DIM), rope.dtype)
    pad = jnp.zeros((rows, heads, HEAD_LANES - QK_NOPE_DIM - QK_ROPE_DIM), rope.dtype)
    return jnp.concatenate([nope, rope, pad], axis=-1).reshape(rows, heads * HEAD_LANES)


def _block_diag_pairs(w_a, w_x):
    blocks, bd, _ = w_a.shape
    eye = jnp.eye(blocks, dtype=w_a.dtype)
    width = blocks * bd
    half = width // 2
    dense = lambda w: jnp.einsum('nde,nm->ndme', w, eye).reshape(width, width)
    da, dx = dense(w_a), dense(w_x)
    tiles = [jnp.concatenate([da[lo:lo + half, lo:lo + half], dx[lo:lo + half, lo:lo + half]], axis=1)
             for lo in (0, half)]
    return jnp.stack(tiles)


def kernel(x, positions, g_ffn1_pre, g_ffn1_post, w_ffn1_gate, w_ffn1_up, w_ffn1_down, g_mix_pre, g_mix_post, w_in, conv_w, conv_b, w_lru_a, b_lru_a, w_lru_x, b_lru_x, lru_lambda, q_a_norm, w_q_b, kv_a_norm, w_kv_b, w_out, g_ffn2_pre, g_ffn2_post, w_ffn2_gate, w_ffn2_up, w_ffn2_down):
    batch, seq, d = x.shape
    depth = g_ffn1_pre.shape[0]
    lru_w = conv_w.shape[-1]
    q_rank = q_a_norm.shape[-1]
    kv_rank = kv_a_norm.shape[-1]
    row = lambda v: v.reshape(1, -1)

    cos_t, sin_t = _rope_tables(positions)
    h = x.reshape(batch * seq, d)
    for l in range(depth):
        h = _ffn(h, row(g_ffn1_pre[l]), w_ffn1_gate[l].astype(BF16), w_ffn1_up[l].astype(BF16),
                 w_ffn1_down[l].astype(BF16), row(g_ffn1_post[l]))

        wi = w_in[l]
        c_rope = 2 * lru_w + q_rank + kv_rank
        k_rope_w = wi[:, c_rope:].reshape(d, 1, QK_ROPE_DIM)
        win = jnp.concatenate([wi[:, :c_rope], _head_slab(None, k_rope_w)], axis=1).astype(BF16)
        wq = w_q_b[l].reshape(q_rank, MLA_HEADS, QK_NOPE_DIM + QK_ROPE_DIM)
        wqa = _head_slab(wq[..., :QK_NOPE_DIM], wq[..., QK_NOPE_DIM:]).T.astype(BF16)
        wkv = w_kv_b[l].reshape(kv_rank, MLA_HEADS, QK_NOPE_DIM + V_HEAD_DIM)
        wk = _head_slab(wkv[..., :QK_NOPE_DIM], jnp.zeros((kv_rank, MLA_HEADS, QK_ROPE_DIM), F32)).astype(BF16)
        wv = wkv[..., QK_NOPE_DIM:].reshape(kv_rank, MLA_HEADS * V_HEAD_DIM).T.astype(BF16)
        wgate = _block_diag_pairs(w_lru_a[l], w_lru_x[l]).astype(BF16)

        q, k, v, y_lru = _mixer_in(h, cos_t, sin_t, batch, row(g_mix_pre[l]), win, conv_w[l], row(conv_b[l]),
                                   wgate, row(b_lru_a[l]), row(b_lru_x[l]), row(lru_lambda[l]),
                                   row(q_a_norm[l]), wqa, row(kv_a_norm[l]), wk, wv)
        late_w = [w_ffn2_gate, w_ffn2_up, w_ffn2_down, w_out]
        if all(_cast_rows(w.shape[1], (batch * seq) // v.shape[2]) for w in late_w):
            y_mla, late_w = _attention(q, k, v, batch, to_bf16=[(w, l) for w in late_w])
        else:
            y_mla, _ = _attention(q, k, v, batch)
            late_w = [w[l].astype(BF16) for w in late_w]
        wg2, wu2, wd2, wo = late_w
        h = _ffn(h, row(g_ffn2_pre[l]), wg2, wu2, wd2, row(g_ffn2_post[l]),
                 mix=(y_lru, y_mla, wo, row(g_mix_post[l])))
    return h.reshape(batch, seq, d)
```

```python
import functools
import math

import jax
import jax.numpy as jnp
from jax import lax
from jax.experimental import pallas as pl
from jax.experimental.pallas import tpu as pltpu

F32 = jnp.float32
BF16 = jnp.bfloat16

EPS = 1e-6
CHUNK = 64
LRU_C = 8.0
ROPE_THETA = 10000.0
MLA_HEADS = 8
QK_NOPE_DIM = 64
QK_ROPE_DIM = 32
V_HEAD_DIM = 64
HEAD_LANES = 128
MASK_VALUE = -1e30

VMEM_LIMIT_BYTES = 56 * 1024 * 1024
TOKEN_TILE = 512
FFN_TOKEN_TILE = 512
FFN_ROW_GROUPS = 2
MIXER_ROW_GROUPS = 1
SUBLANES = 8
LANES = 128
ONES_ROWS = 16
SCORES_AHEAD = 2
LOG2_E = 1.4426950408889634


def _rms(x, g):
    return x * lax.rsqrt(jnp.mean(x * x, axis=-1, keepdims=True) + EPS) * g


def _params(grid_rank, flags=None):
    return pltpu.CompilerParams(
        dimension_semantics=("arbitrary",) * grid_rank, vmem_limit_bytes=VMEM_LIMIT_BYTES, flags=flags)


def _const_spec(shape):
    nd = len(shape)
    return pl.BlockSpec(shape, lambda *_: (0,) * nd, pipeline_mode=pl.Buffered(1))


def _rope_table(pos_ref, invf_ref, cos_ref, sin_ref):
    ang = invf_ref[...] * pos_ref[...].astype(F32)
    cos_ref[...] = jnp.cos(ang)
    sin_ref[...] = jnp.sin(ang)


def _ffn_kernel(*refs, with_mix_out, with_rope, ff_chunks):
    rope_refs = None
    if with_rope:
        rope_refs = (refs[-5], refs[-4], refs[-2], refs[-1])
        refs = refs[:-5] + (refs[-3],)
    if with_mix_out:
        (h_ref, ylru_ref, ymla_ref, wout_ref, gmix_ref,
         gpre_ref, wg_ref, wu_ref, wd_ref, gpost_ref, o_ref) = refs
    else:
        h_ref, gpre_ref, wg_ref, wu_ref, wd_ref, gpost_ref, o_ref = refs
    tm = h_ref.shape[0]
    groups = [slice(r0, r0 + tm // FFN_ROW_GROUPS) for r0 in range(0, tm, tm // FFN_ROW_GROUPS)]
    hs = []
    for rows in groups:
        h = h_ref[rows, :]
        if with_mix_out:
            lru_w = ylru_ref.shape[1]
            m = jnp.dot(ylru_ref[rows, :], wout_ref[:lru_w, :], preferred_element_type=F32)
            m = m + jnp.dot(ymla_ref[rows, :], wout_ref[lru_w:, :], preferred_element_type=F32)
            h = h + _rms(m, gmix_ref[...])
        hs.append(h)
    for idx, (rows, h) in enumerate(zip(groups, hs)):
        if idx == 1 and rope_refs is not None:
            _rope_table(*rope_refs)
        n = _rms(h, gpre_ref[...]).astype(BF16)
        f = None
        for lo, hi in ff_chunks:
            g = jnp.dot(n, wg_ref[:, lo:hi], preferred_element_type=F32)
            u = jnp.dot(n, wu_ref[:, lo:hi], preferred_element_type=F32)
            a = (g * jax.nn.sigmoid(g) * u).astype(BF16)
            part = jnp.dot(a, wd_ref[lo:hi, :], preferred_element_type=F32)
            f = part if f is None else f + part
        o_ref[rows, :] = h + 0.5 * _rms(f, gpost_ref[...])


def _ffn_chunks(d_ff, target=1536):
    chunks, lo = [], 0
    while lo < d_ff:
        hi = min(lo + target, d_ff)
        chunks.append((lo, hi))
        lo = hi
    return tuple(chunks)


def _ffn(h, gpre, wg, wu, wd, gpost, mix=None, positions=None):
    t, d = h.shape
    d_ff = wg.shape[1]
    tm = math.gcd(FFN_TOKEN_TILE, t)
    tok = lambda w: pl.BlockSpec((tm, w), lambda i: (i, 0))
    ins, specs = [h], [tok(d)]
    if mix is not None:
        ylru, ymla, wout, gmix = mix
        ins += [ylru, ymla, wout, gmix]
        specs += [tok(ylru.shape[1]), tok(ymla.shape[1]), _const_spec(wout.shape), _const_spec(gmix.shape)]
    ins += [gpre, wg, wu, wd, gpost]
    specs += [_const_spec(a.shape) for a in (gpre, wg, wu, wd, gpost)]
    out_specs, out_shape = [tok(d)], [jax.ShapeDtypeStruct((t, d), F32)]
    if positions is not None:
        half = QK_ROPE_DIM // 2
        inv_freq = 1.0 / (ROPE_THETA ** (jnp.arange(0, QK_ROPE_DIM, 2, dtype=F32) / QK_ROPE_DIM))
        ins += [positions.reshape(1, t), inv_freq.reshape(half, 1)]
        specs += [pl.BlockSpec((1, tm), lambda i: (0, i)), pl.BlockSpec((half, 1), lambda i: (0, 0))]
        out_specs += [pl.BlockSpec((half, tm), lambda i: (0, i))] * 2
        out_shape += [jax.ShapeDtypeStruct((half, t), F32)] * 2
    outs = pl.pallas_call(
        functools.partial(_ffn_kernel, with_mix_out=mix is not None, with_rope=positions is not None,
                          ff_chunks=_ffn_chunks(d_ff)),
        grid=(t // tm,),
        in_specs=specs,
        out_specs=out_specs,
        out_shape=out_shape,
        compiler_params=_params(1),
        name="ffn_mix_out" if mix is not None else "ffn",
    )(*ins)
    return outs[0] if positions is None else outs


def _mixer_in_kernel(h_ref, cos_ref, sin_ref, *refs, q_scale):
    consts, (q_out, k_out, v_out, ylru_out, xbuf, hbuf, hcar) = refs[:-7], refs[-7:]

    @pl.when(pl.program_id(1) == 0)
    def _():
        xbuf[:, 0:SUBLANES, :] = jnp.zeros((xbuf.shape[0], SUBLANES, LANES), F32)
        hcar[...] = jnp.zeros_like(hcar)

    tm = h_ref.shape[0]
    gm = tm // MIXER_ROW_GROUPS
    for r0 in range(0, tm, gm):
        rows = slice(r0, r0 + gm)
        _mixer_group(h_ref.at[rows, :], cos_ref.at[:, rows], sin_ref.at[:, rows], *consts,
                     q_out.at[:, rows], k_out.at[rows, :], v_out.at[0, :, rows], ylru_out.at[rows, :],
                     xbuf, hbuf, hcar, q_scale=q_scale)


def _mixer_group(h_ref, cos_ref, sin_ref, gpre_ref, win_ref, convw_ref, convb_ref, wgate_ref,
                 ba_ref, bx_ref, lam_ref, qg_ref, wqa_ref, kvg_ref, wk_ref, wv_ref,
                 q_out, k_out, v_out, ylru_out, xbuf, hbuf, hcar, *, q_scale):
    tm = h_ref.shape[0]
    lru_w = ylru_out.shape[1]
    q_rank = qg_ref.shape[1]
    kv_rank = kvg_ref.shape[1]
    heads = q_out.shape[0] // HEAD_LANES
    nt_dims = (((1,), (1,)), ((), ()))

    n = _rms(h_ref[...], gpre_ref[...]).astype(BF16)
    proj = jnp.dot(n, win_ref[...], preferred_element_type=F32)
    c0 = 2 * lru_w
    c1 = c0 + q_rank
    c2 = c1 + kv_rank
    xl = proj[:, :lru_w]
    gate = proj[:, lru_w:c0]
    kpe = proj[:, c2:c2 + HEAD_LANES]
    cos = cos_ref[...]
    sin = sin_ref[...]
    half_r = cos.shape[0]
    r_lo, r_mid, r_hi = QK_NOPE_DIM, QK_NOPE_DIM + half_r, QK_NOPE_DIM + 2 * half_r

    def rope_rows(slab):
        x1, x2 = slab[r_lo:r_mid], slab[r_mid:r_hi]
        return [slab[:r_lo], x1 * cos - x2 * sin, x2 * cos + x1 * sin, slab[r_hi:]]

    slabs = lru_w // LANES
    seg_len = tm // SUBLANES
    pitch = seg_len + SUBLANES
    cw = convw_ref[...]
    cb = convb_ref[...]
    taps = cw.shape[0]
    xc_slabs = []
    for c in range(slabs):
        lanes = slice(c * LANES, (c + 1) * LANES)
        for s in range(SUBLANES):
            lo = SUBLANES + s * pitch
            xbuf[c, lo:lo + seg_len, :] = xl[s * seg_len:(s + 1) * seg_len, lanes]
            if s + 1 < SUBLANES:
                xbuf[c, lo + seg_len:lo + pitch, :] = xl[(s + 1) * seg_len - SUBLANES:(s + 1) * seg_len, lanes]
        x = [xbuf[c, pl.ds(SUBLANES + t, SUBLANES, stride=pitch), :] for t in range(1 - taps, seg_len)]
        xbuf[c, 0:SUBLANES, :] = xl[tm - SUBLANES:, lanes]
        w = [cw[k:k + 1, lanes] for k in range(taps)]
        steps = []
        for t in range(seg_len):
            acc = cb[:, lanes]
            for k in range(taps):
                acc = acc + x[t + k] * w[k]
            steps.append(acc)
        xc_slabs.append(jnp.concatenate(steps, axis=0))
    xc = jnp.concatenate(xc_slabs, axis=1)

    xcb = xc.astype(BF16)
    half = lru_w // 2
    g0 = jnp.dot(xcb[:, :half], wgate_ref[0], preferred_element_type=F32)
    g1 = jnp.dot(xcb[:, half:], wgate_ref[1], preferred_element_type=F32)

    qn = _rms(proj[:, c0:c1], qg_ref[...]).astype(BF16)
    qa = lax.dot_general(wqa_ref[...], qn, nt_dims, preferred_element_type=F32)
    q_rows = []
    for hd in range(heads):
        q_rows += rope_rows(qa[hd * HEAD_LANES:(hd + 1) * HEAD_LANES])
    q_out[...] = (jnp.concatenate(q_rows, axis=0) * q_scale).astype(BF16)

    kvn = _rms(proj[:, c1:c2], kvg_ref[...]).astype(BF16)
    k_rope = jnp.concatenate(rope_rows(kpe.T), axis=0).T
    k_nope = jnp.dot(kvn, wk_ref[...], preferred_element_type=F32)
    k_out[...] = (k_nope + jnp.tile(k_rope, (1, heads))).astype(BF16)
    v_out[...] = lax.dot_general(wv_ref[...], kvn, nt_dims, preferred_element_type=F32).astype(BF16)

    r = jax.nn.sigmoid(jnp.concatenate([g0[:, :half], g1[:, :half]], axis=1) + ba_ref[...])
    i = jax.nn.sigmoid(jnp.concatenate([g0[:, half:], g1[:, half:]], axis=1) + bx_ref[...])
    log_a = (-LRU_C * r) * jax.nn.softplus(-lam_ref[...])
    a = jnp.exp(log_a)
    th = jnp.tanh(log_a)
    sq = -2.0 * th / (1.0 - th)
    u = jnp.where(sq > 0.0, sq * lax.rsqrt(sq), 0.0) * (i * xc)

    step = lambda t, v: v[t * SUBLANES:(t + 1) * SUBLANES]
    end, decay = step(0, u), step(0, a)
    for t in range(1, seg_len):
        end = step(t, a) * end + step(t, u)
        decay = step(t, a) * decay
    seg = lax.broadcasted_iota(jnp.int32, end.shape, 0)
    shift = 1
    while shift < SUBLANES:
        keep = seg >= shift
        end = jnp.where(keep, decay * pltpu.roll(end, shift, 0) + end, end)
        decay = jnp.where(keep, decay * pltpu.roll(decay, shift, 0), decay)
        shift *= 2
    h_in = hcar[...]
    state = jnp.where(seg == 0, h_in, pltpu.roll(end + decay * h_in, 1, 0))
    for t in range(seg_len):
        state = step(t, a) * state + step(t, u)
        for c in range(slabs):
            hbuf[c, pl.ds(t, SUBLANES, stride=pitch), :] = state[:, c * LANES:(c + 1) * LANES]
    hcar[...] = jnp.broadcast_to(state[SUBLANES - 1:SUBLANES, :], state.shape)
    h_lru = jnp.concatenate(
        [jnp.concatenate([hbuf[c, s * pitch:s * pitch + seg_len, :] for s in range(SUBLANES)], axis=0)
         for c in range(slabs)], axis=1)
    ylru_out[...] = (h_lru * jax.nn.gelu(gate)).astype(BF16)


def _mixer_in(h, cos_t, sin_t, batch, gpre, win, convw, convb, wgate, ba, bx, lam, qg, wqa, kvg, wk, wv):
    t, d = h.shape
    seq = t // batch
    tm = min(TOKEN_TILE, seq)
    gm = tm // MIXER_ROW_GROUPS
    ns = seq // tm
    lru_w = convw.shape[1]
    tok = lambda w: pl.BlockSpec((tm, w), lambda b, s: (b * ns + s, 0))
    cols = lambda r: pl.BlockSpec((r, tm), lambda b, s: (0, b * ns + s))
    consts = (gpre, win, convw, convb, wgate, ba, bx, lam, qg, wqa, kvg, wk, wv)
    q_scale = float((QK_NOPE_DIM + QK_ROPE_DIM) ** -0.5 * LOG2_E)
    return pl.pallas_call(
        functools.partial(_mixer_in_kernel, q_scale=q_scale),
        grid=(batch, ns),
        in_specs=[tok(d), cols(cos_t.shape[0]), cols(sin_t.shape[0])] + [_const_spec(a.shape) for a in consts],
        out_specs=[cols(wqa.shape[0]), tok(wk.shape[1]),
                   pl.BlockSpec((1, wv.shape[0], tm), lambda b, s: (b * ns + s, 0, 0)), tok(lru_w)],
        out_shape=[jax.ShapeDtypeStruct((wqa.shape[0], t), BF16),
                   jax.ShapeDtypeStruct((t, wk.shape[1]), BF16),
                   jax.ShapeDtypeStruct((t // tm, wv.shape[0], tm), BF16),
                   jax.ShapeDtypeStruct((t, lru_w), BF16)],
        scratch_shapes=[pltpu.VMEM((lru_w // LANES, gm + SUBLANES * SUBLANES + SUBLANES, LANES), F32),
                        pltpu.VMEM((lru_w // LANES, gm + SUBLANES * SUBLANES, LANES), F32),
                        pltpu.VMEM((SUBLANES, lru_w), F32)],
        compiler_params=_params(2),
        name="mixer_in",
    )(h, cos_t, sin_t, *consts)


def _attn_kernel(q_ref, k_ref, vt_ref, *refs, n_cast):
    cast_in, o_ref, cast_out = refs[:n_cast], refs[n_cast], refs[n_cast + 1:2 * n_cast + 1]
    m_ref, acc_ref, s_ref = refs[2 * n_cast + 1:]
    for src, dst in zip(cast_in, cast_out):
        dst[...] = src[...].astype(dst.dtype)

    tile = q_ref.shape[1]
    half = tile // 2
    heads = q_ref.shape[0] // HEAD_LANES
    qi = pl.program_id(1)

    m_ref[...] = jnp.full(m_ref.shape, MASK_VALUE, F32)
    acc_ref[...] = jnp.zeros_like(acc_ref)
    ones_rows = jnp.ones((ONES_ROWS, tile), BF16)
    slab = lambda h: slice(h * HEAD_LANES, (h + 1) * HEAD_LANES)

    def v_rows(kj, h):
        return jnp.concatenate([vt_ref[kj, h * V_HEAD_DIM:(h + 1) * V_HEAD_DIM, :], ones_rows], axis=0)

    def rescale(h, m_cur):
        m_prev = m_ref[h]
        m_new = jnp.maximum(m_prev, m_cur)
        m_ref[h] = m_new
        return m_new, jnp.exp2(m_prev - m_new)

    def full_scores(kj, h, slot):
        kh = k_ref[pl.ds(pl.multiple_of(kj * tile, tile), tile), slab(h)]
        s_ref[slot] = jnp.dot(kh, q_ref[slab(h), :], preferred_element_type=F32)
        return jnp.max(s_ref[slot], axis=0, keepdims=True)

    def full_accumulate(kj, h, slot, m_cur):
        m_new, alpha = rescale(h, m_cur)
        pt = jnp.exp2(s_ref[slot] - m_new).astype(BF16)
        acc_ref[h] = acc_ref[h] * alpha + jnp.dot(v_rows(kj, h), pt, preferred_element_type=F32)

    def diag_scores(kj, h, slot):
        key_chunk = lax.broadcasted_iota(jnp.int32, (half, half), 0) // CHUNK
        query_chunk = lax.broadcasted_iota(jnp.int32, (half, half), 1) // CHUNK
        visible = key_chunk <= query_chunk
        row0 = pl.multiple_of(kj * tile, tile)
        qh = q_ref[slab(h), :]
        early = jnp.dot(k_ref[pl.ds(row0, half), slab(h)], qh, preferred_element_type=F32)
        late = jnp.dot(k_ref[pl.ds(row0 + half, half), slab(h)], qh[:, half:], preferred_element_type=F32)
        s_ref[slot, :half, :half] = jnp.where(visible, early[:, :half], MASK_VALUE)
        s_ref[slot, :half, half:] = early[:, half:]
        s_ref[slot, half:, half:] = jnp.where(visible, late, MASK_VALUE)
        m_left = jnp.max(s_ref[slot, :half, :half], axis=0, keepdims=True)
        m_right = jnp.maximum(jnp.max(s_ref[slot, :half, half:], axis=0, keepdims=True),
                              jnp.max(s_ref[slot, half:, half:], axis=0, keepdims=True))
        return jnp.concatenate([m_left, m_right], axis=1)

    def diag_accumulate(kj, h, slot, m_cur):
        m_new, alpha = rescale(h, m_cur)
        p_early = jnp.exp2(s_ref[slot, :half, :] - m_new).astype(BF16)
        p_late = jnp.exp2(s_ref[slot, half:, half:] - m_new[:, half:]).astype(BF16)
        vt = v_rows(kj, h)
        pv = jnp.dot(vt[:, :half], p_early, preferred_element_type=F32)
        pv_late = jnp.dot(vt[:, half:], p_late, preferred_element_type=F32)
        acc = acc_ref[h] * alpha + pv
        acc_ref[h] = jnp.concatenate([acc[:, :half], acc[:, half:] + pv_late], axis=1)

    def run(tiles):
        units = [(kj, h, diag) for kj, diag in tiles for h in range(heads)]

        def issue(n):
            kj, h, diag = units[n]
            return (diag_scores if diag else full_scores)(kj, h, n % heads)

        pending = [issue(n) for n in range(min(SCORES_AHEAD, len(units)))]
        for n, (kj, h, diag) in enumerate(units):
            if n + SCORES_AHEAD < len(units):
                pending.append(issue(n + SCORES_AHEAD))
            (diag_accumulate if diag else full_accumulate)(kj, h, n % heads, pending.pop(0))

    def tile_pair(j, carry):
        run([(2 * j, False), (2 * j + 1, False)])
        return carry

    lax.fori_loop(0, qi // 2, tile_pair, 0)

    @pl.when(qi % 2 == 1)
    def _():
        run([(qi - 1, False)])

    run([(qi, True)])

    outs = []
    for h in range(heads):
        a = acc_ref[h]
        outs.append(a[:V_HEAD_DIM] / a[V_HEAD_DIM:V_HEAD_DIM + 1])
    o_ref[...] = jnp.concatenate(outs, axis=0).T.astype(o_ref.dtype)


def _attention(qt, k, vt, batch, to_bf16=()):
    t = k.shape[0]
    seq = t // batch
    tile = vt.shape[2]
    assert tile % (2 * CHUNK) == 0 and seq % tile == 0
    nt = seq // tile
    heads = k.shape[1] // HEAD_LANES
    d_v = vt.shape[1]
    steps = batch * nt
    cast_in_specs, cast_out_specs = [], []
    for w, layer in to_bf16:
        rows = _cast_rows(w.shape[1], steps)
        last = w.shape[1] // rows - 1
        cast_in_specs.append(pl.BlockSpec(
            (None, rows, w.shape[2]), lambda b, i, layer=layer, last=last: (layer, jnp.minimum(b * nt + i, last), 0)))
        cast_out_specs.append(pl.BlockSpec(
            (rows, w.shape[2]), lambda b, i, last=last: (jnp.minimum(b * nt + i, last), 0)))
    outs = pl.pallas_call(
        functools.partial(_attn_kernel, n_cast=len(to_bf16)),
        grid=(batch, nt),
        in_specs=[pl.BlockSpec((qt.shape[0], tile), lambda b, i: (0, b * nt + i)),
                  pl.BlockSpec((seq, k.shape[1]), lambda b, i: (b, 0)),
                  pl.BlockSpec((nt, d_v, tile), lambda b, i: (b, 0, 0))] + cast_in_specs,
        out_specs=[pl.BlockSpec((tile, d_v), lambda b, i: (b * nt + i, 0))] + cast_out_specs,
        scratch_shapes=[pltpu.VMEM((heads, 1, tile), F32),
                        pltpu.VMEM((heads, V_HEAD_DIM + ONES_ROWS, tile), F32),
                        pltpu.VMEM((heads, tile, tile), F32)],
        out_shape=[jax.ShapeDtypeStruct((t, d_v), BF16)]
                  + [jax.ShapeDtypeStruct(w.shape[1:], BF16) for w, _ in to_bf16],
        compiler_params=_params(2),
        name="attention",
    )(qt, k, vt, *[w for w, _ in to_bf16])
    return outs[0], outs[1:]


def _cast_rows(n_rows, steps):
    for rows in range(2 * SUBLANES, n_rows + 1, 2 * SUBLANES):
        if n_rows % rows == 0 and n_rows // rows <= steps:
            return rows
    return None


def _head_slab(nope, rope):
    rows, heads = rope.shape[0], rope.shape[1]
    if nope is None:
        nope = jnp.zeros((rows, heads, QK_NOPE_DIM), rope.dtype)
    pad = jnp.zeros((rows, heads, HEAD_LANES - QK_NOPE_DIM - QK_ROPE_DIM), rope.dtype)
    return jnp.concatenate([nope, rope, pad], axis=-1).reshape(rows, heads * HEAD_LANES)


def _block_diag_pairs(w_a, w_x):
    blocks, bd, _ = w_a.shape
    eye = jnp.eye(blocks, dtype=w_a.dtype)
    width = blocks * bd
    half = width // 2
    dense = lambda w: jnp.einsum('nde,nm->ndme', w, eye).reshape(width, width)
    da, dx = dense(w_a), dense(w_x)
    tiles = [jnp.concatenate([da[lo:lo + half, lo:lo + half], dx[lo:lo + half, lo:lo + half]], axis=1)
             for lo in (0, half)]
    return jnp.stack(tiles)


def kernel(x, positions, g_ffn1_pre, g_ffn1_post, w_ffn1_gate, w_ffn1_up, w_ffn1_down, g_mix_pre, g_mix_post, w_in, conv_w, conv_b, w_lru_a, b_lru_a, w_lru_x, b_lru_x, lru_lambda, q_a_norm, w_q_b, kv_a_norm, w_kv_b, w_out, g_ffn2_pre, g_ffn2_post, w_ffn2_gate, w_ffn2_up, w_ffn2_down):
    batch, seq, d = x.shape
    depth = g_ffn1_pre.shape[0]
    lru_w = conv_w.shape[-1]
    q_rank = q_a_norm.shape[-1]
    kv_rank = kv_a_norm.shape[-1]
    row = lambda v: v.reshape(1, -1)

    h = x.reshape(batch * seq, d)
    for l in range(depth):
        ffn1 = _ffn(h, row(g_ffn1_pre[l]), w_ffn1_gate[l].astype(BF16), w_ffn1_up[l].astype(BF16),
                    w_ffn1_down[l].astype(BF16), row(g_ffn1_post[l]),
                    positions=positions.reshape(-1) if l == 0 else None)
        if l == 0:
            h, cos_t, sin_t = ffn1
        else:
            h = ffn1

        wi = w_in[l]
        c_rope = 2 * lru_w + q_rank + kv_rank
        k_rope_w = wi[:, c_rope:].reshape(d, 1, QK_ROPE_DIM)
        win = jnp.concatenate([wi[:, :c_rope], _head_slab(None, k_rope_w)], axis=1).astype(BF16)
        wq = w_q_b[l].reshape(q_rank, MLA_HEADS, QK_NOPE_DIM + QK_ROPE_DIM)
        wqa = _head_slab(wq[..., :QK_NOPE_DIM], wq[..., QK_NOPE_DIM:]).T.astype(BF16)
        wkv = w_kv_b[l].reshape(kv_rank, MLA_HEADS, QK_NOPE_DIM + V_HEAD_DIM)
        wk = _head_slab(wkv[..., :QK_NOPE_DIM], jnp.zeros((kv_rank, MLA_HEADS, QK_ROPE_DIM), F32)).astype(BF16)
        wv = wkv[..., QK_NOPE_DIM:].reshape(kv_rank, MLA_HEADS * V_HEAD_DIM).T.astype(BF16)
        wgate = _block_diag_pairs(w_lru_a[l], w_lru_x[l]).astype(BF16)

        q, k, v, y_lru = _mixer_in(h, cos_t, sin_t, batch, row(g_mix_pre[l]), win, conv_w[l], row(conv_b[l]),
                                   wgate, row(b_lru_a[l]), row(b_lru_x[l]), row(lru_lambda[l]),
                                   row(q_a_norm[l]), wqa, row(kv_a_norm[l]), wk, wv)
        late_w = [w_ffn2_gate, w_ffn2_up, w_ffn2_down, w_out]
        if all(_cast_rows(w.shape[1], (batch * seq) // v.shape[2]) for w in late_w):
            y_mla, late_w = _attention(q, k, v, batch, to_bf16=[(w, l) for w in late_w])
        else:
            y_mla, _ = _attention(q, k, v, batch)
            late_w = [w[l].astype(BF16) for w in late_w]
        wg2, wu2, wd2, wo = late_w
        h = _ffn(h, row(g_ffn2_pre[l]), wg2, wu2, wd2, row(g_ffn2_post[l]),
                 mix=(y_lru, y_mla, wo, row(g_mix_post[l])))
    return h.reshape(batch, seq, d)
```

```python
import functools
import math

import jax
import jax.numpy as jnp
from jax import lax
from jax.experimental import pallas as pl
from jax.experimental.pallas import tpu as pltpu

F32 = jnp.float32
BF16 = jnp.bfloat16

EPS = 1e-6
CHUNK = 64
LRU_C = 8.0
ROPE_THETA = 10000.0
MLA_HEADS = 8
QK_NOPE_DIM = 64
QK_ROPE_DIM = 32
V_HEAD_DIM = 64
HEAD_LANES = 128
MASK_VALUE = -1e30

VMEM_LIMIT_BYTES = 56 * 1024 * 1024
TOKEN_TILE = 512
FFN_TOKEN_TILE = 512
FFN_ROW_GROUPS = 2
MIXER_ROW_GROUPS = 1
SUBLANES = 8
LANES = 128
ONES_ROWS = 16
SCORES_AHEAD = 2
LOG2_E = 1.4426950408889634


def _rms(x, g):
    return x * lax.rsqrt(jnp.mean(x * x, axis=-1, keepdims=True) + EPS) * g


def _params(grid_rank):
    return pltpu.CompilerParams(dimension_semantics=("arbitrary",) * grid_rank, vmem_limit_bytes=VMEM_LIMIT_BYTES)


def _const_spec(shape):
    nd = len(shape)
    return pl.BlockSpec(shape, lambda *_: (0,) * nd, pipeline_mode=pl.Buffered(1))


def _rope_table(pos_ref, invf_ref, cos_ref, sin_ref):
    ang = invf_ref[...] * pos_ref[...].astype(F32)
    cos_ref[...] = jnp.cos(ang)
    sin_ref[...] = jnp.sin(ang)


def _ffn_kernel(*refs, with_mix_out, with_rope, ff_chunks):
    rope_refs = None
    if with_rope:
        rope_refs = (refs[-5], refs[-4], refs[-2], refs[-1])
        refs = refs[:-5] + (refs[-3],)
    if with_mix_out:
        (h_ref, ylru_ref, ymla_ref, wout_ref, gmix_ref,
         gpre_ref, wg_ref, wu_ref, wd_ref, gpost_ref, o_ref) = refs
    else:
        h_ref, gpre_ref, wg_ref, wu_ref, wd_ref, gpost_ref, o_ref = refs
    tm = h_ref.shape[0]
    groups = [slice(r0, r0 + tm // FFN_ROW_GROUPS) for r0 in range(0, tm, tm // FFN_ROW_GROUPS)]
    hs = []
    for rows in groups:
        h = h_ref[rows, :]
        if with_mix_out:
            lru_w = ylru_ref.shape[1]
            m = jnp.dot(ylru_ref[rows, :], wout_ref[:lru_w, :], preferred_element_type=F32)
            m = m + jnp.dot(ymla_ref[rows, :], wout_ref[lru_w:, :], preferred_element_type=F32)
            h = h + _rms(m, gmix_ref[...])
        hs.append(h)
    for idx, (rows, h) in enumerate(zip(groups, hs)):
        if idx == 1 and rope_refs is not None:
            _rope_table(*rope_refs)
        n = _rms(h, gpre_ref[...]).astype(BF16)
        f = None
        for lo, hi in ff_chunks:
            g = jnp.dot(n, wg_ref[:, lo:hi], preferred_element_type=F32)
            u = jnp.dot(n, wu_ref[:, lo:hi], preferred_element_type=F32)
            a = (g * jax.nn.sigmoid(g) * u).astype(BF16)
            part = jnp.dot(a, wd_ref[lo:hi, :], preferred_element_type=F32)
            f = part if f is None else f + part
        o_ref[rows, :] = h + 0.5 * _rms(f, gpost_ref[...])


def _ffn_chunks(d_ff, target=1536):
    chunks, lo = [], 0
    while lo < d_ff:
        hi = min(lo + target, d_ff)
        chunks.append((lo, hi))
        lo = hi
    return tuple(chunks)


def _ffn(h, gpre, wg, wu, wd, gpost, mix=None, positions=None):
    t, d = h.shape
    d_ff = wg.shape[1]
    tm = math.gcd(FFN_TOKEN_TILE, t)
    tok = lambda w: pl.BlockSpec((tm, w), lambda i: (i, 0))
    ins, specs = [h], [tok(d)]
    if mix is not None:
        ylru, ymla, wout, gmix = mix
        ins += [ylru, ymla, wout, gmix]
        specs += [tok(ylru.shape[1]), tok(ymla.shape[1]), _const_spec(wout.shape), _const_spec(gmix.shape)]
    ins += [gpre, wg, wu, wd, gpost]
    specs += [_const_spec(a.shape) for a in (gpre, wg, wu, wd, gpost)]
    out_specs, out_shape = [tok(d)], [jax.ShapeDtypeStruct((t, d), F32)]
    if positions is not None:
        half = QK_ROPE_DIM // 2
        inv_freq = 1.0 / (ROPE_THETA ** (jnp.arange(0, QK_ROPE_DIM, 2, dtype=F32) / QK_ROPE_DIM))
        ins += [positions.reshape(1, t), inv_freq.reshape(half, 1)]
        specs += [pl.BlockSpec((1, tm), lambda i: (0, i)), pl.BlockSpec((half, 1), lambda i: (0, 0))]
        out_specs += [pl.BlockSpec((half, tm), lambda i: (0, i))] * 2
        out_shape += [jax.ShapeDtypeStruct((half, t), F32)] * 2
    outs = pl.pallas_call(
        functools.partial(_ffn_kernel, with_mix_out=mix is not None, with_rope=positions is not None,
                          ff_chunks=_ffn_chunks(d_ff)),
        grid=(t // tm,),
        in_specs=specs,
        out_specs=out_specs,
        out_shape=out_shape,
        compiler_params=_params(1),
        name="ffn_mix_out" if mix is not None else "ffn",
    )(*ins)
    return outs[0] if positions is None else outs


def _mixer_in_kernel(h_ref, cos_ref, sin_ref, *refs, q_scale):
    consts, (q_out, k_out, v_out, ylru_out, xbuf, hbuf, hcar) = refs[:-7], refs[-7:]

    @pl.when(pl.program_id(1) == 0)
    def _():
        xbuf[:, 0:SUBLANES, :] = jnp.zeros((xbuf.shape[0], SUBLANES, LANES), F32)
        hcar[...] = jnp.zeros_like(hcar)

    tm = h_ref.shape[0]
    gm = tm // MIXER_ROW_GROUPS
    for r0 in range(0, tm, gm):
        rows = slice(r0, r0 + gm)
        _mixer_group(h_ref.at[rows, :], cos_ref.at[:, rows], sin_ref.at[:, rows], *consts,
                     q_out.at[:, rows], k_out.at[rows, :], v_out.at[0, :, rows], ylru_out.at[rows, :],
                     xbuf, hbuf, hcar, q_scale=q_scale)


def _mixer_group(h_ref, cos_ref, sin_ref, gpre_ref, win_ref, convw_ref, convb_ref, wgate_ref,
                 ba_ref, bx_ref, lam_ref, qg_ref, wqa_ref, kvg_ref, wk_ref, wv_ref,
                 q_out, k_out, v_out, ylru_out, xbuf, hbuf, hcar, *, q_scale):
    tm = h_ref.shape[0]
    lru_w = ylru_out.shape[1]
    q_rank = qg_ref.shape[1]
    kv_rank = kvg_ref.shape[1]
    heads = q_out.shape[0] // HEAD_LANES
    nt_dims = (((1,), (1,)), ((), ()))

    n = _rms(h_ref[...], gpre_ref[...]).astype(BF16)
    proj = jnp.dot(n, win_ref[...], preferred_element_type=F32)
    c0 = 2 * lru_w
    c1 = c0 + q_rank
    c2 = c1 + kv_rank
    xl = proj[:, :lru_w]
    gate = proj[:, lru_w:c0]
    kpe = proj[:, c2:c2 + HEAD_LANES]
    cos = cos_ref[...]
    sin = sin_ref[...]
    half_r = cos.shape[0]
    r_lo, r_mid, r_hi = QK_NOPE_DIM, QK_NOPE_DIM + half_r, QK_NOPE_DIM + 2 * half_r

    def rope_rows(slab):
        x1, x2 = slab[r_lo:r_mid], slab[r_mid:r_hi]
        return [slab[:r_lo], x1 * cos - x2 * sin, x2 * cos + x1 * sin, slab[r_hi:]]

    slabs = lru_w // LANES
    seg_len = tm // SUBLANES
    pitch = seg_len + SUBLANES
    cw = convw_ref[...]
    cb = convb_ref[...]
    taps = cw.shape[0]
    xc_slabs = []
    for c in range(slabs):
        lanes = slice(c * LANES, (c + 1) * LANES)
        for s in range(SUBLANES):
            lo = SUBLANES + s * pitch
            xbuf[c, lo:lo + seg_len, :] = xl[s * seg_len:(s + 1) * seg_len, lanes]
            if s + 1 < SUBLANES:
                xbuf[c, lo + seg_len:lo + pitch, :] = xl[(s + 1) * seg_len - SUBLANES:(s + 1) * seg_len, lanes]
        x = [xbuf[c, pl.ds(SUBLANES + t, SUBLANES, stride=pitch), :] for t in range(1 - taps, seg_len)]
        xbuf[c, 0:SUBLANES, :] = xl[tm - SUBLANES:, lanes]
        w = [cw[k:k + 1, lanes] for k in range(taps)]
        steps = []
        for t in range(seg_len):
            acc = cb[:, lanes]
            for k in range(taps):
                acc = acc + x[t + k] * w[k]
            steps.append(acc)
        xc_slabs.append(jnp.concatenate(steps, axis=0))
    xc = jnp.concatenate(xc_slabs, axis=1)

    xcb = xc.astype(BF16)
    half = lru_w // 2
    g0 = jnp.dot(xcb[:, :half], wgate_ref[0], preferred_element_type=F32)
    g1 = jnp.dot(xcb[:, half:], wgate_ref[1], preferred_element_type=F32)

    qn = _rms(proj[:, c0:c1], qg_ref[...]).astype(BF16)
    qa = lax.dot_general(wqa_ref[...], qn, nt_dims, preferred_element_type=F32)
    q_rows = []
    for hd in range(heads):
        q_rows += rope_rows(qa[hd * HEAD_LANES:(hd + 1) * HEAD_LANES])
    q_out[...] = (jnp.concatenate(q_rows, axis=0) * q_scale).astype(BF16)

    kvn = _rms(proj[:, c1:c2], kvg_ref[...]).astype(BF16)
    k_rope = jnp.concatenate(rope_rows(kpe.T), axis=0).T
    k_nope = jnp.dot(kvn, wk_ref[...], preferred_element_type=F32)
    k_out[...] = (k_nope + jnp.tile(k_rope, (1, heads))).astype(BF16)
    v_out[...] = lax.dot_general(wv_ref[...], kvn, nt_dims, preferred_element_type=F32).astype(BF16)

    sigmoid = lambda v: 0.5 * jnp.tanh(0.5 * v) + 0.5
    r = sigmoid(jnp.concatenate([g0[:, :half], g1[:, :half]], axis=1) + ba_ref[...])
    i = sigmoid(jnp.concatenate([g0[:, half:], g1[:, half:]], axis=1) + bx_ref[...])
    log_a = (-LRU_C * r) * jax.nn.softplus(-lam_ref[...])
    a = jnp.exp(log_a)
    th = jnp.tanh(log_a)
    sq = -2.0 * th / (1.0 - th)
    u = jnp.where(sq > 0.0, sq * lax.rsqrt(sq), 0.0) * (i * xc)

    step = lambda t, v: v[t * SUBLANES:(t + 1) * SUBLANES]
    end, decay = step(0, u), step(0, a)
    for t in range(1, seg_len):
        end = step(t, a) * end + step(t, u)
        decay = step(t, a) * decay
    seg = lax.broadcasted_iota(jnp.int32, end.shape, 0)
    shift = 1
    while shift < SUBLANES:
        keep = seg >= shift
        end = jnp.where(keep, decay * pltpu.roll(end, shift, 0) + end, end)
        decay = jnp.where(keep, decay * pltpu.roll(decay, shift, 0), decay)
        shift *= 2
    h_in = hcar[...]
    state = jnp.where(seg == 0, h_in, pltpu.roll(end + decay * h_in, 1, 0))
    for t in range(seg_len):
        state = step(t, a) * state + step(t, u)
        for c in range(slabs):
            hbuf[c, pl.ds(t, SUBLANES, stride=pitch), :] = state[:, c * LANES:(c + 1) * LANES]
    hcar[...] = jnp.broadcast_to(state[SUBLANES - 1:SUBLANES, :], state.shape)
    h_lru = jnp.concatenate(
        [jnp.concatenate([hbuf[c, s * pitch:s * pitch + seg_len, :] for s in range(SUBLANES)], axis=0)
         for c in range(slabs)], axis=1)
    ylru_out[...] = (h_lru * jax.nn.gelu(gate)).astype(BF16)


def _mixer_in(h, cos_t, sin_t, batch, gpre, win, convw, convb, wgate, ba, bx, lam, qg, wqa, kvg, wk, wv):
    t, d = h.shape
    seq = t // batch
    tm = min(TOKEN_TILE, seq)
    gm = tm // MIXER_ROW_GROUPS
    ns = seq // tm
    lru_w = convw.shape[1]
    tok = lambda w: pl.BlockSpec((tm, w), lambda b, s: (b * ns + s, 0))
    cols = lambda r: pl.BlockSpec((r, tm), lambda b, s: (0, b * ns + s))
    consts = (gpre, win, convw, convb, wgate, ba, bx, lam, qg, wqa, kvg, wk, wv)
    q_scale = float((QK_NOPE_DIM + QK_ROPE_DIM) ** -0.5 * LOG2_E)
    return pl.pallas_call(
        functools.partial(_mixer_in_kernel, q_scale=q_scale),
        grid=(batch, ns),
        in_specs=[tok(d), cols(cos_t.shape[0]), cols(sin_t.shape[0])] + [_const_spec(a.shape) for a in consts],
        out_specs=[cols(wqa.shape[0]), tok(wk.shape[1]),
                   pl.BlockSpec((1, wv.shape[0], tm), lambda b, s: (b * ns + s, 0, 0)), tok(lru_w)],
        out_shape=[jax.ShapeDtypeStruct((wqa.shape[0], t), BF16),
                   jax.ShapeDtypeStruct((t, wk.shape[1]), BF16),
                   jax.ShapeDtypeStruct((t // tm, wv.shape[0], tm), BF16),
                   jax.ShapeDtypeStruct((t, lru_w), BF16)],
        scratch_shapes=[pltpu.VMEM((lru_w // LANES, gm + SUBLANES * SUBLANES + SUBLANES, LANES), F32),
                        pltpu.VMEM((lru_w // LANES, gm + SUBLANES * SUBLANES, LANES), F32),
                        pltpu.VMEM((SUBLANES, lru_w), F32)],
        compiler_params=_params(2),
        name="mixer_in",
    )(h, cos_t, sin_t, *consts)


def _attn_kernel(q_ref, k_ref, vt_ref, *refs, n_cast):
    cast_in, o_ref, cast_out = refs[:n_cast], refs[n_cast], refs[n_cast + 1:2 * n_cast + 1]
    m_ref, acc_ref, s_ref = refs[2 * n_cast + 1:]
    for src, dst in zip(cast_in, cast_out):
        dst[...] = src[...].astype(dst.dtype)

    tile = q_ref.shape[1]
    half = tile // 2
    heads = q_ref.shape[0] // HEAD_LANES
    qi = pl.program_id(1)

    m_ref[...] = jnp.full(m_ref.shape, MASK_VALUE, F32)
    acc_ref[...] = jnp.zeros_like(acc_ref)
    ones_rows = jnp.ones((ONES_ROWS, tile), BF16)
    slab = lambda h: slice(h * HEAD_LANES, (h + 1) * HEAD_LANES)

    def v_rows(kj, h):
        return jnp.concatenate([vt_ref[kj, h * V_HEAD_DIM:(h + 1) * V_HEAD_DIM, :], ones_rows], axis=0)

    def rescale(h, m_cur):
        m_prev = m_ref[h]
        m_new = jnp.maximum(m_prev, m_cur)
        m_ref[h] = m_new
        return m_new, jnp.exp2(m_prev - m_new)

    def full_scores(kj, h, slot):
        kh = k_ref[pl.ds(pl.multiple_of(kj * tile, tile), tile), slab(h)]
        s_ref[slot] = jnp.dot(kh, q_ref[slab(h), :], preferred_element_type=F32)
        return jnp.max(s_ref[slot], axis=0, keepdims=True)

    def full_accumulate(kj, h, slot, m_cur):
        m_new, alpha = rescale(h, m_cur)
        pt = jnp.exp2(s_ref[slot] - m_new).astype(BF16)
        acc_ref[h] = acc_ref[h] * alpha + jnp.dot(v_rows(kj, h), pt, preferred_element_type=F32)

    def diag_scores(kj, h, slot):
        key_chunk = lax.broadcasted_iota(jnp.int32, (half, half), 0) // CHUNK
        query_chunk = lax.broadcasted_iota(jnp.int32, (half, half), 1) // CHUNK
        visible = key_chunk <= query_chunk
        row0 = pl.multiple_of(kj * tile, tile)
        qh = q_ref[slab(h), :]
        early = jnp.dot(k_ref[pl.ds(row0, half), slab(h)], qh, preferred_element_type=F32)
        late = jnp.dot(k_ref[pl.ds(row0 + half, half), slab(h)], qh[:, half:], preferred_element_type=F32)
        s_ref[slot, :half, :half] = jnp.where(visible, early[:, :half], MASK_VALUE)
        s_ref[slot, :half, half:] = early[:, half:]
        s_ref[slot, half:, half:] = jnp.where(visible, late, MASK_VALUE)
        m_left = jnp.max(s_ref[slot, :half, :half], axis=0, keepdims=True)
        m_right = jnp.maximum(jnp.max(s_ref[slot, :half, half:], axis=0, keepdims=True),
                              jnp.max(s_ref[slot, half:, half:], axis=0, keepdims=True))
        return jnp.concatenate([m_left, m_right], axis=1)

    def diag_accumulate(kj, h, slot, m_cur):
        m_new, alpha = rescale(h, m_cur)
        p_early = jnp.exp2(s_ref[slot, :half, :] - m_new).astype(BF16)
        p_late = jnp.exp2(s_ref[slot, half:, half:] - m_new[:, half:]).astype(BF16)
        vt = v_rows(kj, h)
        pv = jnp.dot(vt[:, :half], p_early, preferred_element_type=F32)
        pv_late = jnp.dot(vt[:, half:], p_late, preferred_element_type=F32)
        acc = acc_ref[h] * alpha + pv
        acc_ref[h] = jnp.concatenate([acc[:, :half], acc[:, half:] + pv_late], axis=1)

    def run(tiles):
        units = [(kj, h, diag) for kj, diag in tiles for h in range(heads)]

        def issue(n):
            kj, h, diag = units[n]
            return (diag_scores if diag else full_scores)(kj, h, n % heads)

        pending = [issue(n) for n in range(min(SCORES_AHEAD, len(units)))]
        for n, (kj, h, diag) in enumerate(units):
            if n + SCORES_AHEAD < len(units):
                pending.append(issue(n + SCORES_AHEAD))
            (diag_accumulate if diag else full_accumulate)(kj, h, n % heads, pending.pop(0))

    def tile_pair(j, carry):
        run([(2 * j, False), (2 * j + 1, False)])
        return carry

    lax.fori_loop(0, qi // 2, tile_pair, 0)

    @pl.when(qi % 2 == 1)
    def _():
        run([(qi - 1, False)])

    run([(qi, True)])

    outs = []
    for h in range(heads):
        a = acc_ref[h]
        outs.append(a[:V_HEAD_DIM] / a[V_HEAD_DIM:V_HEAD_DIM + 1])
    o_ref[...] = jnp.concatenate(outs, axis=0).T.astype(o_ref.dtype)


def _attention(qt, k, vt, batch, to_bf16=()):
    t = k.shape[0]
    seq = t // batch
    tile = vt.shape[2]
    assert tile % (2 * CHUNK) == 0 and seq % tile == 0
    nt = seq // tile
    heads = k.shape[1] // HEAD_LANES
    d_v = vt.shape[1]
    steps = batch * nt
    cast_in_specs, cast_out_specs = [], []
    for w, layer in to_bf16:
        rows = _cast_rows(w.shape[1], steps)
        last = w.shape[1] // rows - 1
        cast_in_specs.append(pl.BlockSpec(
            (None, rows, w.shape[2]), lambda b, i, layer=layer, last=last: (layer, jnp.minimum(b * nt + i, last), 0)))
        cast_out_specs.append(pl.BlockSpec(
            (rows, w.shape[2]), lambda b, i, last=last: (jnp.minimum(b * nt + i, last), 0)))
    outs = pl.pallas_call(
        functools.partial(_attn_kernel, n_cast=len(to_bf16)),
        grid=(batch, nt),
        in_specs=[pl.BlockSpec((qt.shape[0], tile), lambda b, i: (0, b * nt + i)),
                  pl.BlockSpec((seq, k.shape[1]), lambda b, i: (b, 0)),
                  pl.BlockSpec((nt, d_v, tile), lambda b, i: (b, 0, 0))] + cast_in_specs,
        out_specs=[pl.BlockSpec((tile, d_v), lambda b, i: (b * nt + i, 0))] + cast_out_specs,
        scratch_shapes=[pltpu.VMEM((heads, 1, tile), F32),
                        pltpu.VMEM((heads, V_HEAD_DIM + ONES_ROWS, tile), F32),
                        pltpu.VMEM((heads, tile, tile), F32)],
        out_shape=[jax.ShapeDtypeStruct((t, d_v), BF16)]
                  + [jax.ShapeDtypeStruct(w.shape[1:], BF16) for w, _ in to_bf16],
        compiler_params=_params(2),
        name="attention",
    )(qt, k, vt, *[w for w, _ in to_bf16])
    return outs[0], outs[1:]


def _cast_rows(n_rows, steps):
    for rows in range(2 * SUBLANES, n_rows + 1, 2 * SUBLANES):
        if n_rows % rows == 0 and n_rows // rows <= steps:
            return rows
    return None


def _head_slab(nope, rope):
    rows, heads = rope.shape[0], rope.shape[1]
    if nope is None:
        nope = jnp.zeros((rows, heads, QK_NOPE_DIM), rope.dtype)
    pad = jnp.zeros((rows, heads, HEAD_LANES - QK_NOPE_DIM - QK_ROPE_DIM), rope.dtype)
    return jnp.concatenate([nope, rope, pad], axis=-1).reshape(rows, heads * HEAD_LANES)


def _block_diag_pairs(w_a, w_x):
    blocks, bd, _ = w_a.shape
    eye = jnp.eye(blocks, dtype=w_a.dtype)
    width = blocks * bd
    half = width // 2
    dense = lambda w: jnp.einsum('nde,nm->ndme', w, eye).reshape(width, width)
    da, dx = dense(w_a), dense(w_x)
    tiles = [jnp.concatenate([da[lo:lo + half, lo:lo + half], dx[lo:lo + half, lo:lo + half]], axis=1)
             for lo in (0, half)]
    return jnp.stack(tiles)


def kernel(x, positions, g_ffn1_pre, g_ffn1_post, w_ffn1_gate, w_ffn1_up, w_ffn1_down, g_mix_pre, g_mix_post, w_in, conv_w, conv_b, w_lru_a, b_lru_a, w_lru_x, b_lru_x, lru_lambda, q_a_norm, w_q_b, kv_a_norm, w_kv_b, w_out, g_ffn2_pre, g_ffn2_post, w_ffn2_gate, w_ffn2_up, w_ffn2_down):
    batch, seq, d = x.shape
    depth = g_ffn1_pre.shape[0]
    lru_w = conv_w.shape[-1]
    q_rank = q_a_norm.shape[-1]
    kv_rank = kv_a_norm.shape[-1]
    row = lambda v: v.reshape(1, -1)

    h = x.reshape(batch * seq, d)
    for l in range(depth):
        ffn1 = _ffn(h, row(g_ffn1_pre[l]), w_ffn1_gate[l].astype(BF16), w_ffn1_up[l].astype(BF16),
                    w_ffn1_down[l].astype(BF16), row(g_ffn1_post[l]),
                    positions=positions.reshape(-1) if l == 0 else None)
        if l == 0:
            h, cos_t, sin_t = ffn1
        else:
            h = ffn1

        wi = w_in[l]
        c_rope = 2 * lru_w + q_rank + kv_rank
        k_rope_w = wi[:, c_rope:].reshape(d, 1, QK_ROPE_DIM)
        win = jnp.concatenate([wi[:, :c_rope], _head_slab(None, k_rope_w)], axis=1).astype(BF16)
        wq = w_q_b[l].reshape(q_rank, MLA_HEADS, QK_NOPE_DIM + QK_ROPE_DIM)
        wqa = _head_slab(wq[..., :QK_NOPE_DIM], wq[..., QK_NOPE_DIM:]).T.astype(BF16)
        wkv = w_kv_b[l].reshape(kv_rank, MLA_HEADS, QK_NOPE_DIM + V_HEAD_DIM)
        wk = _head_slab(wkv[..., :QK_NOPE_DIM], jnp.zeros((kv_rank, MLA_HEADS, QK_ROPE_DIM), F32)).astype(BF16)
        wv = wkv[..., QK_NOPE_DIM:].reshape(kv_rank, MLA_HEADS * V_HEAD_DIM).T.astype(BF16)
        wgate = _block_diag_pairs(w_lru_a[l], w_lru_x[l]).astype(BF16)

        q, k, v, y_lru = _mixer_in(h, cos_t, sin_t, batch, row(g_mix_pre[l]), win, conv_w[l], row(conv_b[l]),
                                   wgate, row(b_lru_a[l]), row(b_lru_x[l]), row(lru_lambda[l]),
                                   row(q_a_norm[l]), wqa, row(kv_a_norm[l]), wk, wv)
        late_w = [w_ffn2_gate, w_ffn2_up, w_ffn2_down, w_out]
        if all(_cast_rows(w.shape[1], (batch * seq) // v.shape[2]) for w in late_w):
            y_mla, late_w = _attention(q, k, v, batch, to_bf16=[(w, l) for w in late_w])
        else:
            y_mla, _ = _attention(q, k, v, batch)
            late_w = [w[l].astype(BF16) for w in late_w]
        wg2, wu2, wd2, wo = late_w
        h = _ffn(h, row(g_ffn2_pre[l]), wg2, wu2, wd2, row(g_ffn2_post[l]),
                 mix=(y_lru, y_mla, wo, row(g_mix_post[l])))
    return h.reshape(batch, seq, d)
```

```python
import functools
import math

import jax
import jax.numpy as jnp
from jax import lax
from jax.experimental import pallas as pl
from jax.experimental.pallas import tpu as pltpu

F32 = jnp.float32
BF16 = jnp.bfloat16

EPS = 1e-6
CHUNK = 64
LRU_C = 8.0
ROPE_THETA = 10000.0
MLA_HEADS = 8
QK_NOPE_DIM = 64
QK_ROPE_DIM = 32
V_HEAD_DIM = 64
HEAD_LANES = 128
MASK_VALUE = -1e30

VMEM_LIMIT_BYTES = 56 * 1024 * 1024
TOKEN_TILE = 512
FFN_TOKEN_TILE = 512
FFN_ROW_GROUPS = 2
MIXER_ROW_GROUPS = 1
SUBLANES = 8
LANES = 128
ONES_ROWS = 16
SCORES_AHEAD = 2
LOG2_E = 1.4426950408889634


def _rms(x, g):
    return x * lax.rsqrt(jnp.mean(x * x, axis=-1, keepdims=True) + EPS) * g


def _sigmoid(x):
    return 0.5 * jnp.tanh(0.5 * x) + 0.5


def _params(grid_rank):
    return pltpu.CompilerParams(dimension_semantics=("arbitrary",) * grid_rank, vmem_limit_bytes=VMEM_LIMIT_BYTES)


def _const_spec(shape):
    nd = len(shape)
    return pl.BlockSpec(shape, lambda *_: (0,) * nd, pipeline_mode=pl.Buffered(1))


def _rope_table(pos_ref, invf_ref, cos_ref, sin_ref):
    ang = invf_ref[...] * pos_ref[...].astype(F32)
    cos_ref[...] = jnp.cos(ang)
    sin_ref[...] = jnp.sin(ang)


def _ffn_kernel(*refs, with_mix_out, with_rope, ff_chunks):
    rope_refs = None
    if with_rope:
        rope_refs = (refs[-5], refs[-4], refs[-2], refs[-1])
        refs = refs[:-5] + (refs[-3],)
    if with_mix_out:
        (h_ref, ylru_ref, ymla_ref, wout_ref, gmix_ref,
         gpre_ref, wg_ref, wu_ref, wd_ref, gpost_ref, o_ref) = refs
    else:
        h_ref, gpre_ref, wg_ref, wu_ref, wd_ref, gpost_ref, o_ref = refs
    tm = h_ref.shape[0]
    groups = [slice(r0, r0 + tm // FFN_ROW_GROUPS) for r0 in range(0, tm, tm // FFN_ROW_GROUPS)]
    hs = []
    for rows in groups:
        h = h_ref[rows, :]
        if with_mix_out:
            lru_w = ylru_ref.shape[1]
            m = jnp.dot(ylru_ref[rows, :], wout_ref[:lru_w, :], preferred_element_type=F32)
            m = m + jnp.dot(ymla_ref[rows, :], wout_ref[lru_w:, :], preferred_element_type=F32)
            h = h + _rms(m, gmix_ref[...])
        hs.append(h)
    for idx, (rows, h) in enumerate(zip(groups, hs)):
        if idx == 1 and rope_refs is not None:
            _rope_table(*rope_refs)
        n = _rms(h, gpre_ref[...]).astype(BF16)
        f = None
        for lo, hi in ff_chunks:
            g = jnp.dot(n, wg_ref[:, lo:hi], preferred_element_type=F32)
            u = jnp.dot(n, wu_ref[:, lo:hi], preferred_element_type=F32)
            a = (g * _sigmoid(g) * u).astype(BF16)
            part = jnp.dot(a, wd_ref[lo:hi, :], preferred_element_type=F32)
            f = part if f is None else f + part
        o_ref[rows, :] = h + 0.5 * _rms(f, gpost_ref[...])


def _ffn_chunks(d_ff, target=1536):
    chunks, lo = [], 0
    while lo < d_ff:
        hi = min(lo + target, d_ff)
        chunks.append((lo, hi))
        lo = hi
    return tuple(chunks)


def _ffn(h, gpre, wg, wu, wd, gpost, mix=None, positions=None):
    t, d = h.shape
    d_ff = wg.shape[1]
    tm = math.gcd(FFN_TOKEN_TILE, t)
    tok = lambda w: pl.BlockSpec((tm, w), lambda i: (i, 0))
    ins, specs = [h], [tok(d)]
    if mix is not None:
        ylru, ymla, wout, gmix = mix
        ins += [ylru, ymla, wout, gmix]
        specs += [tok(ylru.shape[1]), tok(ymla.shape[1]), _const_spec(wout.shape), _const_spec(gmix.shape)]
    ins += [gpre, wg, wu, wd, gpost]
    specs += [_const_spec(a.shape) for a in (gpre, wg, wu, wd, gpost)]
    out_specs, out_shape = [tok(d)], [jax.ShapeDtypeStruct((t, d), F32)]
    if positions is not None:
        half = QK_ROPE_DIM // 2
        inv_freq = 1.0 / (ROPE_THETA ** (jnp.arange(0, QK_ROPE_DIM, 2, dtype=F32) / QK_ROPE_DIM))
        ins += [positions.reshape(1, t), inv_freq.reshape(half, 1)]
        specs += [pl.BlockSpec((1, tm), lambda i: (0, i)), pl.BlockSpec((half, 1), lambda i: (0, 0))]
        out_specs += [pl.BlockSpec((half, tm), lambda i: (0, i))] * 2
        out_shape += [jax.ShapeDtypeStruct((half, t), F32)] * 2
    outs = pl.pallas_call(
        functools.partial(_ffn_kernel, with_mix_out=mix is not None, with_rope=positions is not None,
                          ff_chunks=_ffn_chunks(d_ff)),
        grid=(t // tm,),
        in_specs=specs,
        out_specs=out_specs,
        out_shape=out_shape,
        compiler_params=_params(1),
        name="ffn_mix_out" if mix is not None else "ffn",
    )(*ins)
    return outs[0] if positions is None else outs


def _mixer_in_kernel(h_ref, cos_ref, sin_ref, *refs, q_scale):
    consts, (q_out, k_out, v_out, ylru_out, xbuf, hbuf, hcar) = refs[:-7], refs[-7:]

    @pl.when(pl.program_id(1) == 0)
    def _():
        xbuf[:, 0:SUBLANES, :] = jnp.zeros((xbuf.shape[0], SUBLANES, LANES), F32)
        hcar[...] = jnp.zeros_like(hcar)

    tm = h_ref.shape[0]
    gm = tm // MIXER_ROW_GROUPS
    for r0 in range(0, tm, gm):
        rows = slice(r0, r0 + gm)
        _mixer_group(h_ref.at[rows, :], cos_ref.at[:, rows], sin_ref.at[:, rows], *consts,
                     q_out.at[:, rows], k_out.at[rows, :], v_out.at[0, :, rows], ylru_out.at[rows, :],
                     xbuf, hbuf, hcar, q_scale=q_scale)


def _mixer_group(h_ref, cos_ref, sin_ref, gpre_ref, win_ref, convw_ref, convb_ref, wgate_ref,
                 ba_ref, bx_ref, lam_ref, qg_ref, wqa_ref, kvg_ref, wk_ref, wv_ref,
                 q_out, k_out, v_out, ylru_out, xbuf, hbuf, hcar, *, q_scale):
    tm = h_ref.shape[0]
    lru_w = ylru_out.shape[1]
    q_rank = qg_ref.shape[1]
    kv_rank = kvg_ref.shape[1]
    heads = q_out.shape[0] // HEAD_LANES
    nt_dims = (((1,), (1,)), ((), ()))

    n = _rms(h_ref[...], gpre_ref[...]).astype(BF16)
    proj = jnp.dot(n, win_ref[...], preferred_element_type=F32)
    c0 = 2 * lru_w
    c1 = c0 + q_rank
    c2 = c1 + kv_rank
    xl = proj[:, :lru_w]
    gate = proj[:, lru_w:c0]
    kpe = proj[:, c2:c2 + HEAD_LANES]
    cos = cos_ref[...]
    sin = sin_ref[...]
    half_r = cos.shape[0]
    r_lo, r_mid, r_hi = QK_NOPE_DIM, QK_NOPE_DIM + half_r, QK_NOPE_DIM + 2 * half_r

    def rope_rows(slab):
        x1, x2 = slab[r_lo:r_mid], slab[r_mid:r_hi]
        return [slab[:r_lo], x1 * cos - x2 * sin, x2 * cos + x1 * sin, slab[r_hi:]]

    slabs = lru_w // LANES
    seg_len = tm // SUBLANES
    pitch = seg_len + SUBLANES
    cw = convw_ref[...]
    cb = convb_ref[...]
    taps = cw.shape[0]
    xc_slabs = []
    for c in range(slabs):
        lanes = slice(c * LANES, (c + 1) * LANES)
        for s in range(SUBLANES):
            lo = SUBLANES + s * pitch
            xbuf[c, lo:lo + seg_len, :] = xl[s * seg_len:(s + 1) * seg_len, lanes]
            if s + 1 < SUBLANES:
                xbuf[c, lo + seg_len:lo + pitch, :] = xl[(s + 1) * seg_len - SUBLANES:(s + 1) * seg_len, lanes]
        x = [xbuf[c, pl.ds(SUBLANES + t, SUBLANES, stride=pitch), :] for t in range(1 - taps, seg_len)]
        xbuf[c, 0:SUBLANES, :] = xl[tm - SUBLANES:, lanes]
        w = [cw[k:k + 1, lanes] for k in range(taps)]
        steps = []
        for t in range(seg_len):
            acc = cb[:, lanes]
            for k in range(taps):
                acc = acc + x[t + k] * w[k]
            steps.append(acc)
        xc_slabs.append(jnp.concatenate(steps, axis=0))
    xc = jnp.concatenate(xc_slabs, axis=1)

    xcb = xc.astype(BF16)
    half = lru_w // 2
    g0 = jnp.dot(xcb[:, :half], wgate_ref[0], preferred_element_type=F32)
    g1 = jnp.dot(xcb[:, half:], wgate_ref[1], preferred_element_type=F32)

    qn = _rms(proj[:, c0:c1], qg_ref[...]).astype(BF16)
    qa = lax.dot_general(wqa_ref[...], qn, nt_dims, preferred_element_type=F32)
    q_rows = []
    for hd in range(heads):
        q_rows += rope_rows(qa[hd * HEAD_LANES:(hd + 1) * HEAD_LANES])
    q_out[...] = (jnp.concatenate(q_rows, axis=0) * q_scale).astype(BF16)

    kvn = _rms(proj[:, c1:c2], kvg_ref[...]).astype(BF16)
    k_rope = jnp.concatenate(rope_rows(kpe.T), axis=0).T
    k_nope = jnp.dot(kvn, wk_ref[...], preferred_element_type=F32)
    k_out[...] = (k_nope + jnp.tile(k_rope, (1, heads))).astype(BF16)
    v_out[...] = lax.dot_general(wv_ref[...], kvn, nt_dims, preferred_element_type=F32).astype(BF16)

    r = _sigmoid(jnp.concatenate([g0[:, :half], g1[:, :half]], axis=1) + ba_ref[...])
    i = _sigmoid(jnp.concatenate([g0[:, half:], g1[:, half:]], axis=1) + bx_ref[...])
    log_a = (-LRU_C * r) * jax.nn.softplus(-lam_ref[...])
    a = jnp.exp(log_a)
    th = jnp.tanh(log_a)
    sq = -2.0 * th / (1.0 - th)
    u = jnp.where(sq > 0.0, sq * lax.rsqrt(sq), 0.0) * (i * xc)

    step = lambda t, v: v[t * SUBLANES:(t + 1) * SUBLANES]
    end, decay = step(0, u), step(0, a)
    for t in range(1, seg_len):
        end = step(t, a) * end + step(t, u)
        decay = step(t, a) * decay
    seg = lax.broadcasted_iota(jnp.int32, end.shape, 0)
    shift = 1
    while shift < SUBLANES:
        keep = seg >= shift
        end = jnp.where(keep, decay * pltpu.roll(end, shift, 0) + end, end)
        decay = jnp.where(keep, decay * pltpu.roll(decay, shift, 0), decay)
        shift *= 2
    h_in = hcar[...]
    state = jnp.where(seg == 0, h_in, pltpu.roll(end + decay * h_in, 1, 0))
    for t in range(seg_len):
        state = step(t, a) * state + step(t, u)
        for c in range(slabs):
            hbuf[c, pl.ds(t, SUBLANES, stride=pitch), :] = state[:, c * LANES:(c + 1) * LANES]
    hcar[...] = jnp.broadcast_to(state[SUBLANES - 1:SUBLANES, :], state.shape)
    h_lru = jnp.concatenate(
        [jnp.concatenate([hbuf[c, s * pitch:s * pitch + seg_len, :] for s in range(SUBLANES)], axis=0)
         for c in range(slabs)], axis=1)
    ylru_out[...] = (h_lru * jax.nn.gelu(gate)).astype(BF16)


def _mixer_in(h, cos_t, sin_t, batch, gpre, win, convw, convb, wgate, ba, bx, lam, qg, wqa, kvg, wk, wv):
    t, d = h.shape
    seq = t // batch
    tm = min(TOKEN_TILE, seq)
    gm = tm // MIXER_ROW_GROUPS
    ns = seq // tm
    lru_w = convw.shape[1]
    tok = lambda w: pl.BlockSpec((tm, w), lambda b, s: (b * ns + s, 0))
    cols = lambda r: pl.BlockSpec((r, tm), lambda b, s: (0, b * ns + s))
    consts = (gpre, win, convw, convb, wgate, ba, bx, lam, qg, wqa, kvg, wk, wv)
    q_scale = float((QK_NOPE_DIM + QK_ROPE_DIM) ** -0.5 * LOG2_E)
    return pl.pallas_call(
        functools.partial(_mixer_in_kernel, q_scale=q_scale),
        grid=(batch, ns),
        in_specs=[tok(d), cols(cos_t.shape[0]), cols(sin_t.shape[0])] + [_const_spec(a.shape) for a in consts],
        out_specs=[cols(wqa.shape[0]), tok(wk.shape[1]),
                   pl.BlockSpec((1, wv.shape[0], tm), lambda b, s: (b * ns + s, 0, 0)), tok(lru_w)],
        out_shape=[jax.ShapeDtypeStruct((wqa.shape[0], t), BF16),
                   jax.ShapeDtypeStruct((t, wk.shape[1]), BF16),
                   jax.ShapeDtypeStruct((t // tm, wv.shape[0], tm), BF16),
                   jax.ShapeDtypeStruct((t, lru_w), BF16)],
        scratch_shapes=[pltpu.VMEM((lru_w // LANES, gm + SUBLANES * SUBLANES + SUBLANES, LANES), F32),
                        pltpu.VMEM((lru_w // LANES, gm + SUBLANES * SUBLANES, LANES), F32),
                        pltpu.VMEM((SUBLANES, lru_w), F32)],
        compiler_params=_params(2),
        name="mixer_in",
    )(h, cos_t, sin_t, *consts)


def _attn_kernel(q_ref, k_ref, vt_ref, *refs, n_cast):
    cast_in, o_ref, cast_out = refs[:n_cast], refs[n_cast], refs[n_cast + 1:2 * n_cast + 1]
    m_ref, acc_ref, s_ref = refs[2 * n_cast + 1:]
    for src, dst in zip(cast_in, cast_out):
        dst[...] = src[...].astype(dst.dtype)

    tile = q_ref.shape[1]
    half = tile // 2
    heads = q_ref.shape[0] // HEAD_LANES
    qi = pl.program_id(1)

    m_ref[...] = jnp.full(m_ref.shape, MASK_VALUE, F32)
    acc_ref[...] = jnp.zeros_like(acc_ref)
    ones_rows = jnp.ones((ONES_ROWS, tile), BF16)
    slab = lambda h: slice(h * HEAD_LANES, (h + 1) * HEAD_LANES)

    def v_rows(kj, h):
        return jnp.concatenate([vt_ref[kj, h * V_HEAD_DIM:(h + 1) * V_HEAD_DIM, :], ones_rows], axis=0)

    def rescale(h, m_cur):
        m_prev = m_ref[h]
        m_new = jnp.maximum(m_prev, m_cur)
        m_ref[h] = m_new
        return m_new, jnp.exp2(m_prev - m_new)

    def full_scores(kj, h, slot):
        kh = k_ref[pl.ds(pl.multiple_of(kj * tile, tile), tile), slab(h)]
        s_ref[slot] = jnp.dot(kh, q_ref[slab(h), :], preferred_element_type=F32)
        return jnp.max(s_ref[slot], axis=0, keepdims=True)

    def full_accumulate(kj, h, slot, m_cur):
        m_new, alpha = rescale(h, m_cur)
        pt = jnp.exp2(s_ref[slot] - m_new).astype(BF16)
        acc_ref[h] = acc_ref[h] * alpha + jnp.dot(v_rows(kj, h), pt, preferred_element_type=F32)

    def diag_scores(kj, h, slot):
        key_chunk = lax.broadcasted_iota(jnp.int32, (half, half), 0) // CHUNK
        query_chunk = lax.broadcasted_iota(jnp.int32, (half, half), 1) // CHUNK
        visible = key_chunk <= query_chunk
        row0 = pl.multiple_of(kj * tile, tile)
        qh = q_ref[slab(h), :]
        early = jnp.dot(k_ref[pl.ds(row0, half), slab(h)], qh, preferred_element_type=F32)
        late = jnp.dot(k_ref[pl.ds(row0 + half, half), slab(h)], qh[:, half:], preferred_element_type=F32)
        s_ref[slot, :half, :half] = jnp.where(visible, early[:, :half], MASK_VALUE)
        s_ref[slot, :half, half:] = early[:, half:]
        s_ref[slot, half:, half:] = jnp.where(visible, late, MASK_VALUE)
        m_left = jnp.max(s_ref[slot, :half, :half], axis=0, keepdims=True)
        m_right = jnp.maximum(jnp.max(s_ref[slot, :half, half:], axis=0, keepdims=True),
                              jnp.max(s_ref[slot, half:, half:], axis=0, keepdims=True))
        return jnp.concatenate([m_left, m_right], axis=1)

    def diag_accumulate(kj, h, slot, m_cur):
        m_new, alpha = rescale(h, m_cur)
        p_early = jnp.exp2(s_ref[slot, :half, :] - m_new).astype(BF16)
        p_late = jnp.exp2(s_ref[slot, half:, half:] - m_new[:, half:]).astype(BF16)
        vt = v_rows(kj, h)
        pv = jnp.dot(vt[:, :half], p_early, preferred_element_type=F32)
        pv_late = jnp.dot(vt[:, half:], p_late, preferred_element_type=F32)
        acc = acc_ref[h] * alpha + pv
        acc_ref[h] = jnp.concatenate([acc[:, :half], acc[:, half:] + pv_late], axis=1)

    def run(tiles):
        units = [(kj, h, diag) for kj, diag in tiles for h in range(heads)]

        def issue(n):
            kj, h, diag = units[n]
            return (diag_scores if diag else full_scores)(kj, h, n % heads)

        pending = [issue(n) for n in range(min(SCORES_AHEAD, len(units)))]
        for n, (kj, h, diag) in enumerate(units):
            if n + SCORES_AHEAD < len(units):
                pending.append(issue(n + SCORES_AHEAD))
            (diag_accumulate if diag else full_accumulate)(kj, h, n % heads, pending.pop(0))

    def tile_pair(j, carry):
        run([(2 * j, False), (2 * j + 1, False)])
        return carry

    lax.fori_loop(0, qi // 2, tile_pair, 0)

    @pl.when(qi % 2 == 1)
    def _():
        run([(qi - 1, False)])

    run([(qi, True)])

    outs = []
    for h in range(heads):
        a = acc_ref[h]
        outs.append(a[:V_HEAD_DIM] / a[V_HEAD_DIM:V_HEAD_DIM + 1])
    o_ref[...] = jnp.concatenate(outs, axis=0).T.astype(o_ref.dtype)


def _attention(qt, k, vt, batch, to_bf16=()):
    t = k.shape[0]
    seq = t // batch
    tile = vt.shape[2]
    assert tile % (2 * CHUNK) == 0 and seq % tile == 0
    nt = seq // tile
    heads = k.shape[1] // HEAD_LANES
    d_v = vt.shape[1]
    steps = batch * nt
    cast_in_specs, cast_out_specs = [], []
    for w, layer in to_bf16:
        rows = _cast_rows(w.shape[1], steps)
        last = w.shape[1] // rows - 1
        cast_in_specs.append(pl.BlockSpec(
            (None, rows, w.shape[2]), lambda b, i, layer=layer, last=last: (layer, jnp.minimum(b * nt + i, last), 0)))
        cast_out_specs.append(pl.BlockSpec(
            (rows, w.shape[2]), lambda b, i, last=last: (jnp.minimum(b * nt + i, last), 0)))
    outs = pl.pallas_call(
        functools.partial(_attn_kernel, n_cast=len(to_bf16)),
        grid=(batch, nt),
        in_specs=[pl.BlockSpec((qt.shape[0], tile), lambda b, i: (0, b * nt + i)),
                  pl.BlockSpec((seq, k.shape[1]), lambda b, i: (b, 0)),
                  pl.BlockSpec((nt, d_v, tile), lambda b, i: (b, 0, 0))] + cast_in_specs,
        out_specs=[pl.BlockSpec((tile, d_v), lambda b, i: (b * nt + i, 0))] + cast_out_specs,
        scratch_shapes=[pltpu.VMEM((heads, 1, tile), F32),
                        pltpu.VMEM((heads, V_HEAD_DIM + ONES_ROWS, tile), F32),
                        pltpu.VMEM((heads, tile, tile), F32)],
        out_shape=[jax.ShapeDtypeStruct((t, d_v), BF16)]
                  + [jax.ShapeDtypeStruct(w.shape[1:], BF16) for w, _ in to_bf16],
        compiler_params=_params(2),
        name="attention",
    )(qt, k, vt, *[w for w, _ in to_bf16])
    return outs[0], outs[1:]


def _cast_rows(n_rows, steps):
    for rows in range(2 * SUBLANES, n_rows + 1, 2 * SUBLANES):
        if n_rows % rows == 0 and n_rows // rows <= steps:
            return rows
    return None


def _head_slab(nope, rope):
    rows, heads = rope.shape[0], rope.shape[1]
    if nope is None:
        nope = jnp.zeros((rows, heads, QK_NOPE_DIM), rope.dtype)
    pad = jnp.zeros((rows, heads, HEAD_LANES - QK_NOPE_DIM - QK_ROPE_DIM), rope.dtype)
    return jnp.concatenate([nope, rope, pad], axis=-1).reshape(rows, heads * HEAD_LANES)


def _block_diag_pairs(w_a, w_x):
    blocks, bd, _ = w_a.shape
    eye = jnp.eye(blocks, dtype=w_a.dtype)
    width = blocks * bd
    half = width // 2
    dense = lambda w: jnp.einsum('nde,nm->ndme', w, eye).reshape(width, width)
    da, dx = dense(w_a), dense(w_x)
    tiles = [jnp.concatenate([da[lo:lo + half, lo:lo + half], dx[lo:lo + half, lo:lo + half]], axis=1)
             for lo in (0, half)]
    return jnp.stack(tiles)


def kernel(x, positions, g_ffn1_pre, g_ffn1_post, w_ffn1_gate, w_ffn1_up, w_ffn1_down, g_mix_pre, g_mix_post, w_in, conv_w, conv_b, w_lru_a, b_lru_a, w_lru_x, b_lru_x, lru_lambda, q_a_norm, w_q_b, kv_a_norm, w_kv_b, w_out, g_ffn2_pre, g_ffn2_post, w_ffn2_gate, w_ffn2_up, w_ffn2_down):
    batch, seq, d = x.shape
    depth = g_ffn1_pre.shape[0]
    lru_w = conv_w.shape[-1]
    q_rank = q_a_norm.shape[-1]
    kv_rank = kv_a_norm.shape[-1]
    row = lambda v: v.reshape(1, -1)

    h = x.reshape(batch * seq, d)
    for l in range(depth):
        ffn1 = _ffn(h, row(g_ffn1_pre[l]), w_ffn1_gate[l].astype(BF16), w_ffn1_up[l].astype(BF16),
                    w_ffn1_down[l].astype(BF16), row(g_ffn1_post[l]),
                    positions=positions.reshape(-1) if l == 0 else None)
        if l == 0:
            h, cos_t, sin_t = ffn1
        else:
            h = ffn1

        wi = w_in[l]
        c_rope = 2 * lru_w + q_rank + kv_rank
        k_rope_w = wi[:, c_rope:].reshape(d, 1, QK_ROPE_DIM)
        win = jnp.concatenate([wi[:, :c_rope], _head_slab(None, k_rope_w)], axis=1).astype(BF16)
        wq = w_q_b[l].reshape(q_rank, MLA_HEADS, QK_NOPE_DIM + QK_ROPE_DIM)
        wqa = _head_slab(wq[..., :QK_NOPE_DIM], wq[..., QK_NOPE_DIM:]).T.astype(BF16)
        wkv = w_kv_b[l].reshape(kv_rank, MLA_HEADS, QK_NOPE_DIM + V_HEAD_DIM)
        wk = _head_slab(wkv[..., :QK_NOPE_DIM], jnp.zeros((kv_rank, MLA_HEADS, QK_ROPE_DIM), F32)).astype(BF16)
        wv = wkv[..., QK_NOPE_DIM:].reshape(kv_rank, MLA_HEADS * V_HEAD_DIM).T.astype(BF16)
        wgate = _block_diag_pairs(w_lru_a[l], w_lru_x[l]).astype(BF16)

        q, k, v, y_lru = _mixer_in(h, cos_t, sin_t, batch, row(g_mix_pre[l]), win, conv_w[l], row(conv_b[l]),
                                   wgate, row(b_lru_a[l]), row(b_lru_x[l]), row(lru_lambda[l]),
                                   row(q_a_norm[l]), wqa, row(kv_a_norm[l]), wk, wv)
        late_w = [w_ffn2_gate, w_ffn2_up, w_ffn2_down, w_out]
        if all(_cast_rows(w.shape[1], (batch * seq) // v.shape[2]) for w in late_w):
            y_mla, late_w = _attention(q, k, v, batch, to_bf16=[(w, l) for w in late_w])
        else:
            y_mla, _ = _attention(q, k, v, batch)
            late_w = [w[l].astype(BF16) for w in late_w]
        wg2, wu2, wd2, wo = late_w
        h = _ffn(h, row(g_ffn2_pre[l]), wg2, wu2, wd2, row(g_ffn2_post[l]),
                 mix=(y_lru, y_mla, wo, row(g_mix_post[l])))
    return h.reshape(batch, seq, d)
```

```python
import functools
import math

import jax
import jax.numpy as jnp
from jax import lax
from jax.experimental import pallas as pl
from jax.experimental.pallas import tpu as pltpu

F32 = jnp.float32
BF16 = jnp.bfloat16

EPS = 1e-6
CHUNK = 64
LRU_C = 8.0
ROPE_THETA = 10000.0
MLA_HEADS = 8
QK_NOPE_DIM = 64
QK_ROPE_DIM = 32
V_HEAD_DIM = 64
HEAD_LANES = 128
MASK_VALUE = -1e30

VMEM_LIMIT_BYTES = 56 * 1024 * 1024
TOKEN_TILE = 512
FFN_TOKEN_TILE = 512
FFN_ROW_GROUPS = 2
MIXER_ROW_GROUPS = 1
SUBLANES = 8
LANES = 128
ONES_ROWS = 16
SCORES_AHEAD = 2
LOG2_E = 1.4426950408889634


def _rms(x, g):
    return x * lax.rsqrt(jnp.mean(x * x, axis=-1, keepdims=True) + EPS) * g


def _params(grid_rank):
    return pltpu.CompilerParams(dimension_semantics=("arbitrary",) * grid_rank, vmem_limit_bytes=VMEM_LIMIT_BYTES)


def _const_spec(shape):
    nd = len(shape)
    return pl.BlockSpec(shape, lambda *_: (0,) * nd, pipeline_mode=pl.Buffered(1))


def _rope_table(pos_ref, invf_ref, cos_ref, sin_ref):
    ang = invf_ref[...] * pos_ref[...].astype(F32)
    cos_ref[...] = jnp.cos(ang)
    sin_ref[...] = jnp.sin(ang)


def _ffn_kernel(*refs, with_mix_out, with_rope, ff_chunks):
    rope_refs = None
    if with_rope:
        rope_refs = (refs[-5], refs[-4], refs[-2], refs[-1])
        refs = refs[:-5] + (refs[-3],)
    if with_mix_out:
        (h_ref, ylru_ref, ymla_ref, wout_ref, gmix_ref,
         gpre_ref, wg_ref, wu_ref, wd_ref, gpost_ref, o_ref) = refs
    else:
        h_ref, gpre_ref, wg_ref, wu_ref, wd_ref, gpost_ref, o_ref = refs
    tm = h_ref.shape[0]
    groups = [slice(r0, r0 + tm // FFN_ROW_GROUPS) for r0 in range(0, tm, tm // FFN_ROW_GROUPS)]
    hs = []
    for rows in groups:
        h = h_ref[rows, :]
        if with_mix_out:
            lru_w = ylru_ref.shape[1]
            m = jnp.dot(ylru_ref[rows, :], wout_ref[:lru_w, :], preferred_element_type=F32)
            m = m + jnp.dot(ymla_ref[rows, :], wout_ref[lru_w:, :], preferred_element_type=F32)
            h = h + _rms(m, gmix_ref[...])
        hs.append(h)
    for idx, (rows, h) in enumerate(zip(groups, hs)):
        if idx == 1 and rope_refs is not None:
            _rope_table(*rope_refs)
        n = _rms(h, gpre_ref[...]).astype(BF16)
        f = None
        for lo, hi in ff_chunks:
            g = jnp.dot(n, wg_ref[:, lo:hi], preferred_element_type=F32)
            u = jnp.dot(n, wu_ref[:, lo:hi], preferred_element_type=F32)
            a = (g * jax.nn.sigmoid(g) * u).astype(BF16)
            part = jnp.dot(a, wd_ref[lo:hi, :], preferred_element_type=F32)
            f = part if f is None else f + part
        o_ref[rows, :] = h + 0.5 * _rms(f, gpost_ref[...])


def _ffn_chunks(d_ff, target=1536):
    chunks, lo = [], 0
    while lo < d_ff:
        hi = min(lo + target, d_ff)
        chunks.append((lo, hi))
        lo = hi
    return tuple(chunks)


def _ffn(h, gpre, wg, wu, wd, gpost, mix=None, positions=None):
    t, d = h.shape
    d_ff = wg.shape[1]
    tm = math.gcd(FFN_TOKEN_TILE, t)
    tok = lambda w: pl.BlockSpec((tm, w), lambda i: (i, 0))
    ins, specs = [h], [tok(d)]
    if mix is not None:
        ylru, ymla, wout, gmix = mix
        ins += [ylru, ymla, wout, gmix]
        specs += [tok(ylru.shape[1]), tok(ymla.shape[1]), _const_spec(wout.shape), _const_spec(gmix.shape)]
    ins += [gpre, wg, wu, wd, gpost]
    specs += [_const_spec(a.shape) for a in (gpre, wg, wu, wd, gpost)]
    out_specs, out_shape = [tok(d)], [jax.ShapeDtypeStruct((t, d), F32)]
    if positions is not None:
        half = QK_ROPE_DIM // 2
        inv_freq = 1.0 / (ROPE_THETA ** (jnp.arange(0, QK_ROPE_DIM, 2, dtype=F32) / QK_ROPE_DIM))
        ins += [positions.reshape(1, t), inv_freq.reshape(half, 1)]
        specs += [pl.BlockSpec((1, tm), lambda i: (0, i)), pl.BlockSpec((half, 1), lambda i: (0, 0))]
        out_specs += [pl.BlockSpec((half, tm), lambda i: (0, i))] * 2
        out_shape += [jax.ShapeDtypeStruct((half, t), F32)] * 2
    outs = pl.pallas_call(
        functools.partial(_ffn_kernel, with_mix_out=mix is not None, with_rope=positions is not None,
                          ff_chunks=_ffn_chunks(d_ff)),
        grid=(t // tm,),
        in_specs=specs,
        out_specs=out_specs,
        out_shape=out_shape,
        compiler_params=_params(1),
        name="ffn_mix_out" if mix is not None else "ffn",
    )(*ins)
    return outs[0] if positions is None else outs


def _mixer_in_kernel(h_ref, cos_ref, sin_ref, *refs, q_scale):
    consts, (q_out, k_out, v_out, ylru_out, xbuf, hbuf, hcar) = refs[:-7], refs[-7:]

    @pl.when(pl.program_id(1) == 0)
    def _():
        xbuf[:, 0:SUBLANES, :] = jnp.zeros((xbuf.shape[0], SUBLANES, LANES), F32)
        hcar[...] = jnp.zeros_like(hcar)

    tm = h_ref.shape[0]
    gm = tm // MIXER_ROW_GROUPS
    for r0 in range(0, tm, gm):
        rows = slice(r0, r0 + gm)
        _mixer_group(h_ref.at[rows, :], cos_ref.at[:, rows], sin_ref.at[:, rows], *consts,
                     q_out.at[:, rows], k_out.at[rows, :], v_out.at[0, :, rows], ylru_out.at[rows, :],
                     xbuf, hbuf, hcar, q_scale=q_scale)


def _mixer_group(h_ref, cos_ref, sin_ref, gpre_ref, win_ref, convw_ref, convb_ref, wgate_ref,
                 ba_ref, bx_ref, lam_ref, qg_ref, wqa_ref, kvg_ref, wk_ref, wv_ref,
                 q_out, k_out, v_out, ylru_out, xbuf, hbuf, hcar, *, q_scale):
    tm = h_ref.shape[0]
    lru_w = ylru_out.shape[1]
    q_rank = qg_ref.shape[1]
    kv_rank = kvg_ref.shape[1]
    heads = q_out.shape[0] // HEAD_LANES
    nt_dims = (((1,), (1,)), ((), ()))

    n = _rms(h_ref[...], gpre_ref[...]).astype(BF16)
    proj = jnp.dot(n, win_ref[...], preferred_element_type=F32)
    c0 = 2 * lru_w
    c1 = c0 + q_rank
    c2 = c1 + kv_rank
    xl = proj[:, :lru_w]
    gate = proj[:, lru_w:c0]
    kpe = proj[:, c2:c2 + HEAD_LANES]
    cos = cos_ref[...]
    sin = sin_ref[...]
    half_r = cos.shape[0]
    r_lo, r_mid, r_hi = QK_NOPE_DIM, QK_NOPE_DIM + half_r, QK_NOPE_DIM + 2 * half_r

    def rope_rows(slab):
        x1, x2 = slab[r_lo:r_mid], slab[r_mid:r_hi]
        return [slab[:r_lo], x1 * cos - x2 * sin, x2 * cos + x1 * sin, slab[r_hi:]]

    slabs = lru_w // LANES
    seg_len = tm // SUBLANES
    pitch = seg_len + SUBLANES
    cw = convw_ref[...]
    cb = convb_ref[...]
    taps = cw.shape[0]
    xc_slabs = []
    for c in range(slabs):
        lanes = slice(c * LANES, (c + 1) * LANES)
        for s in range(SUBLANES):
            lo = SUBLANES + s * pitch
            xbuf[c, lo:lo + seg_len, :] = xl[s * seg_len:(s + 1) * seg_len, lanes]
            if s + 1 < SUBLANES:
                xbuf[c, lo + seg_len:lo + pitch, :] = xl[(s + 1) * seg_len - SUBLANES:(s + 1) * seg_len, lanes]
        x = [xbuf[c, pl.ds(SUBLANES + t, SUBLANES, stride=pitch), :] for t in range(1 - taps, seg_len)]
        xbuf[c, 0:SUBLANES, :] = xl[tm - SUBLANES:, lanes]
        w = [cw[k:k + 1, lanes] for k in range(taps)]
        steps = []
        for t in range(seg_len):
            acc = cb[:, lanes]
            for k in range(taps):
                acc = acc + x[t + k] * w[k]
            steps.append(acc)
        xc_slabs.append(jnp.concatenate(steps, axis=0))
    xc = jnp.concatenate(xc_slabs, axis=1)

    xcb = xc.astype(BF16)
    half = lru_w // 2
    g0 = jnp.dot(xcb[:, :half], wgate_ref[0], preferred_element_type=F32)
    g1 = jnp.dot(xcb[:, half:], wgate_ref[1], preferred_element_type=F32)

    qn = _rms(proj[:, c0:c1], qg_ref[...]).astype(BF16)
    qa = lax.dot_general(wqa_ref[...], qn, nt_dims, preferred_element_type=F32)
    q_rows = []
    for hd in range(heads):
        q_rows += rope_rows(qa[hd * HEAD_LANES:(hd + 1) * HEAD_LANES])
    q_out[...] = (jnp.concatenate(q_rows, axis=0) * q_scale).astype(BF16)

    kvn = _rms(proj[:, c1:c2], kvg_ref[...]).astype(BF16)
    k_rope = jnp.concatenate(rope_rows(kpe.T), axis=0).T
    k_nope = jnp.dot(kvn, wk_ref[...], preferred_element_type=F32)
    k_out[...] = (k_nope + jnp.tile(k_rope, (1, heads))).astype(BF16)
    v_out[...] = lax.dot_general(wv_ref[...], kvn, nt_dims, preferred_element_type=F32).astype(BF16)

    sigmoid = lambda v: 0.5 * jnp.tanh(0.5 * v) + 0.5
    r = sigmoid(jnp.concatenate([g0[:, :half], g1[:, :half]], axis=1) + ba_ref[...])
    i = sigmoid(jnp.concatenate([g0[:, half:], g1[:, half:]], axis=1) + bx_ref[...])
    log_a = (-LRU_C * r) * jax.nn.softplus(-lam_ref[...])
    a = jnp.exp(log_a)
    th = jnp.tanh(log_a)
    sq = -2.0 * th / (1.0 - th)
    u = jnp.where(sq > 0.0, sq * lax.rsqrt(sq), 0.0) * (i * xc)

    step = lambda t, v: v[t * SUBLANES:(t + 1) * SUBLANES]
    end, decay = step(0, u), step(0, a)
    for t in range(1, seg_len):
        end = step(t, a) * end + step(t, u)
        decay = step(t, a) * decay
    seg = lax.broadcasted_iota(jnp.int32, end.shape, 0)
    shift = 1
    while shift < SUBLANES:
        keep = seg >= shift
        end = jnp.where(keep, decay * pltpu.roll(end, shift, 0) + end, end)
        decay = jnp.where(keep, decay * pltpu.roll(decay, shift, 0), decay)
        shift *= 2
    h_in = hcar[...]
    state = jnp.where(seg == 0, h_in, pltpu.roll(end + decay * h_in, 1, 0))
    for t in range(seg_len):
        state = step(t, a) * state + step(t, u)
        for c in range(slabs):
            hbuf[c, pl.ds(t, SUBLANES, stride=pitch), :] = state[:, c * LANES:(c + 1) * LANES]
    hcar[...] = jnp.broadcast_to(state[SUBLANES - 1:SUBLANES, :], state.shape)
    h_lru = jnp.concatenate(
        [jnp.concatenate([hbuf[c, s * pitch:s * pitch + seg_len, :] for s in range(SUBLANES)], axis=0)
         for c in range(slabs)], axis=1)
    ylru_out[...] = (h_lru * jax.nn.gelu(gate)).astype(BF16)


def _mixer_in(h, cos_t, sin_t, batch, gpre, win, convw, convb, wgate, ba, bx, lam, qg, wqa, kvg, wk, wv):
    t, d = h.shape
    seq = t // batch
    tm = min(TOKEN_TILE, seq)
    gm = tm // MIXER_ROW_GROUPS
    ns = seq // tm
    lru_w = convw.shape[1]
    tok = lambda w: pl.BlockSpec((tm, w), lambda b, s: (b * ns + s, 0))
    cols = lambda r: pl.BlockSpec((r, tm), lambda b, s: (0, b * ns + s))
    consts = (gpre, win, convw, convb, wgate, ba, bx, lam, qg, wqa, kvg, wk, wv)
    q_scale = float((QK_NOPE_DIM + QK_ROPE_DIM) ** -0.5 * LOG2_E)
    return pl.pallas_call(
        functools.partial(_mixer_in_kernel, q_scale=q_scale),
        grid=(batch, ns),
        in_specs=[tok(d), cols(cos_t.shape[0]), cols(sin_t.shape[0])] + [_const_spec(a.shape) for a in consts],
        out_specs=[cols(wqa.shape[0]), tok(wk.shape[1]),
                   pl.BlockSpec((1, wv.shape[0], tm), lambda b, s: (b * ns + s, 0, 0)), tok(lru_w)],
        out_shape=[jax.ShapeDtypeStruct((wqa.shape[0], t), BF16),
                   jax.ShapeDtypeStruct((t, wk.shape[1]), BF16),
                   jax.ShapeDtypeStruct((t // tm, wv.shape[0], tm), BF16),
                   jax.ShapeDtypeStruct((t, lru_w), BF16)],
        scratch_shapes=[pltpu.VMEM((lru_w // LANES, gm + SUBLANES * SUBLANES + SUBLANES, LANES), F32),
                        pltpu.VMEM((lru_w // LANES, gm + SUBLANES * SUBLANES, LANES), F32),
                        pltpu.VMEM((SUBLANES, lru_w), F32)],
        compiler_params=_params(2),
        name="mixer_in",
    )(h, cos_t, sin_t, *consts)


def _attn_kernel(q_ref, k_ref, vt_ref, *refs, n_cast):
    cast_in, o_ref, cast_out = refs[:n_cast], refs[n_cast], refs[n_cast + 1:2 * n_cast + 1]
    m_ref, acc_ref, s_ref = refs[2 * n_cast + 1:]
    for src, dst in zip(cast_in, cast_out):
        dst[...] = src[...].astype(dst.dtype)

    tile = q_ref.shape[1]
    half = tile // 2
    heads = q_ref.shape[0] // HEAD_LANES
    qi = pl.program_id(1)

    m_ref[...] = jnp.full(m_ref.shape, MASK_VALUE, F32)
    acc_ref[...] = jnp.zeros_like(acc_ref)
    ones_rows = jnp.ones((ONES_ROWS, tile), BF16)
    slab = lambda h: slice(h * HEAD_LANES, (h + 1) * HEAD_LANES)

    def v_rows(kj, h):
        return jnp.concatenate([vt_ref[kj, h * V_HEAD_DIM:(h + 1) * V_HEAD_DIM, :], ones_rows], axis=0)

    def rescale(h, m_cur):
        m_prev = m_ref[h]
        m_new = jnp.maximum(m_prev, m_cur)
        m_ref[h] = m_new
        return m_new, jnp.exp2(m_prev - m_new)

    def full_scores(kj, h, slot):
        kh = k_ref[pl.ds(pl.multiple_of(kj * tile, tile), tile), slab(h)]
        s_ref[slot] = jnp.dot(kh, q_ref[slab(h), :], preferred_element_type=F32)
        return jnp.max(s_ref[slot], axis=0, keepdims=True)

    def full_accumulate(kj, h, slot, m_cur):
        m_new, alpha = rescale(h, m_cur)
        pt = jnp.exp2(s_ref[slot] - m_new).astype(BF16)
        acc_ref[h] = acc_ref[h] * alpha + jnp.dot(v_rows(kj, h), pt, preferred_element_type=F32)

    def diag_scores(kj, h, slot):
        key_chunk = lax.broadcasted_iota(jnp.int32, (half, half), 0) // CHUNK
        query_chunk = lax.broadcasted_iota(jnp.int32, (half, half), 1) // CHUNK
        visible = key_chunk <= query_chunk
        row0 = pl.multiple_of(kj * tile, tile)
        qh = q_ref[slab(h), :]
        early = jnp.dot(k_ref[pl.ds(row0, half), slab(h)], qh, preferred_element_type=F32)
        late = jnp.dot(k_ref[pl.ds(row0 + half, half), slab(h)], qh[:, half:], preferred_element_type=F32)
        s_ref[slot, :half, :half] = jnp.where(visible, early[:, :half], MASK_VALUE)
        s_ref[slot, :half, half:] = early[:, half:]
        s_ref[slot, half:, half:] = jnp.where(visible, late, MASK_VALUE)
        m_left = jnp.max(s_ref[slot, :half, :half], axis=0, keepdims=True)
        m_right = jnp.maximum(jnp.max(s_ref[slot, :half, half:], axis=0, keepdims=True),
                              jnp.max(s_ref[slot, half:, half:], axis=0, keepdims=True))
        return jnp.concatenate([m_left, m_right], axis=1)

    def diag_accumulate(kj, h, slot, m_cur):
        m_new, alpha = rescale(h, m_cur)
        p_early = jnp.exp2(s_ref[slot, :half, :] - m_new).astype(BF16)
        p_late = jnp.exp2(s_ref[slot, half:, half:] - m_new[:, half:]).astype(BF16)
        vt = v_rows(kj, h)
        pv = jnp.dot(vt[:, :half], p_early, preferred_element_type=F32)
        pv_late = jnp.dot(vt[:, half:], p_late, preferred_element_type=F32)
        acc = acc_ref[h] * alpha + pv
        acc_ref[h] = jnp.concatenate([acc[:, :half], acc[:, half:] + pv_late], axis=1)

    def run(tiles):
        units = [(kj, h, diag) for kj, diag in tiles for h in range(heads)]

        def issue(n):
            kj, h, diag = units[n]
            return (diag_scores if diag else full_scores)(kj, h, n % heads)

        pending = [issue(n) for n in range(min(SCORES_AHEAD, len(units)))]
        for n, (kj, h, diag) in enumerate(units):
            if n + SCORES_AHEAD < len(units):
                pending.append(issue(n + SCORES_AHEAD))
            (diag_accumulate if diag else full_accumulate)(kj, h, n % heads, pending.pop(0))

    def tile_quad(j, carry):
        run([(4 * j + t, False) for t in range(4)])
        return carry

    lax.fori_loop(0, qi // 4, tile_quad, 0)

    @pl.when(qi % 4 >= 2)
    def _():
        first = (qi // 4) * 4
        run([(first, False), (first + 1, False)])

    @pl.when(qi % 2 == 1)
    def _():
        run([(qi - 1, False)])

    run([(qi, True)])

    outs = []
    for h in range(heads):
        a = acc_ref[h]
        outs.append(a[:V_HEAD_DIM] / a[V_HEAD_DIM:V_HEAD_DIM + 1])
    o_ref[...] = jnp.concatenate(outs, axis=0).T.astype(o_ref.dtype)


def _attention(qt, k, vt, batch, to_bf16=()):
    t = k.shape[0]
    seq = t // batch
    tile = vt.shape[2]
    assert tile % (2 * CHUNK) == 0 and seq % tile == 0
    nt = seq // tile
    heads = k.shape[1] // HEAD_LANES
    d_v = vt.shape[1]
    steps = batch * nt
    cast_in_specs, cast_out_specs = [], []
    for w, layer in to_bf16:
        rows = _cast_rows(w.shape[1], steps)
        last = w.shape[1] // rows - 1
        cast_in_specs.append(pl.BlockSpec(
            (None, rows, w.shape[2]), lambda b, i, layer=layer, last=last: (layer, jnp.minimum(b * nt + i, last), 0)))
        cast_out_specs.append(pl.BlockSpec(
            (rows, w.shape[2]), lambda b, i, last=last: (jnp.minimum(b * nt + i, last), 0)))
    outs = pl.pallas_call(
        functools.partial(_attn_kernel, n_cast=len(to_bf16)),
        grid=(batch, nt),
        in_specs=[pl.BlockSpec((qt.shape[0], tile), lambda b, i: (0, b * nt + i)),
                  pl.BlockSpec((seq, k.shape[1]), lambda b, i: (b, 0)),
                  pl.BlockSpec((nt, d_v, tile), lambda b, i: (b, 0, 0))] + cast_in_specs,
        out_specs=[pl.BlockSpec((tile, d_v), lambda b, i: (b * nt + i, 0))] + cast_out_specs,
        scratch_shapes=[pltpu.VMEM((heads, 1, tile), F32),
                        pltpu.VMEM((heads, V_HEAD_DIM + ONES_ROWS, tile), F32),
                        pltpu.VMEM((heads, tile, tile), F32)],
        out_shape=[jax.ShapeDtypeStruct((t, d_v), BF16)]
                  + [jax.ShapeDtypeStruct(w.shape[1:], BF16) for w, _ in to_bf16],
        compiler_params=_params(2),
        name="attention",
    )(qt, k, vt, *[w for w, _ in to_bf16])
    return outs[0], outs[1:]


def _cast_rows(n_rows, steps):
    for rows in range(2 * SUBLANES, n_rows + 1, 2 * SUBLANES):
        if n_rows % rows == 0 and n_rows // rows <= steps:
            return rows
    return None


def _head_slab(nope, rope):
    rows, heads = rope.shape[0], rope.shape[1]
    if nope is None:
        nope = jnp.zeros((rows, heads, QK_NOPE_DIM), rope.dtype)
    pad = jnp.zeros((rows, heads, HEAD_LANES - QK_NOPE_DIM - QK_ROPE_DIM), rope.dtype)
    return jnp.concatenate([nope, rope, pad], axis=-1).reshape(rows, heads * HEAD_LANES)


def _block_diag_pairs(w_a, w_x):
    blocks, bd, _ = w_a.shape
    eye = jnp.eye(blocks, dtype=w_a.dtype)
    width = blocks * bd
    half = width // 2
    dense = lambda w: jnp.einsum('nde,nm->ndme', w, eye).reshape(width, width)
    da, dx = dense(w_a), dense(w_x)
    tiles = [jnp.concatenate([da[lo:lo + half, lo:lo + half], dx[lo:lo + half, lo:lo + half]], axis=1)
             for lo in (0, half)]
    return jnp.stack(tiles)


def kernel(x, positions, g_ffn1_pre, g_ffn1_post, w_ffn1_gate, w_ffn1_up, w_ffn1_down, g_mix_pre, g_mix_post, w_in, conv_w, conv_b, w_lru_a, b_lru_a, w_lru_x, b_lru_x, lru_lambda, q_a_norm, w_q_b, kv_a_norm, w_kv_b, w_out, g_ffn2_pre, g_ffn2_post, w_ffn2_gate, w_ffn2_up, w_ffn2_down):
    batch, seq, d = x.shape
    depth = g_ffn1_pre.shape[0]
    lru_w = conv_w.shape[-1]
    q_rank = q_a_norm.shape[-1]
    kv_rank = kv_a_norm.shape[-1]
    row = lambda v: v.reshape(1, -1)

    h = x.reshape(batch * seq, d)
    for l in range(depth):
        ffn1 = _ffn(h, row(g_ffn1_pre[l]), w_ffn1_gate[l].astype(BF16), w_ffn1_up[l].astype(BF16),
                    w_ffn1_down[l].astype(BF16), row(g_ffn1_post[l]),
                    positions=positions.reshape(-1) if l == 0 else None)
        if l == 0:
            h, cos_t, sin_t = ffn1
        else:
            h = ffn1

        wi = w_in[l]
        c_rope = 2 * lru_w + q_rank + kv_rank
        k_rope_w = wi[:, c_rope:].reshape(d, 1, QK_ROPE_DIM)
        win = jnp.concatenate([wi[:, :c_rope], _head_slab(None, k_rope_w)], axis=1).astype(BF16)
        wq = w_q_b[l].reshape(q_rank, MLA_HEADS, QK_NOPE_DIM + QK_ROPE_DIM)
        wqa = _head_slab(wq[..., :QK_NOPE_DIM], wq[..., QK_NOPE_DIM:]).T.astype(BF16)
        wkv = w_kv_b[l].reshape(kv_rank, MLA_HEADS, QK_NOPE_DIM + V_HEAD_DIM)
        wk = _head_slab(wkv[..., :QK_NOPE_DIM], jnp.zeros((kv_rank, MLA_HEADS, QK_ROPE_DIM), F32)).astype(BF16)
        wv = wkv[..., QK_NOPE_DIM:].reshape(kv_rank, MLA_HEADS * V_HEAD_DIM).T.astype(BF16)
        wgate = _block_diag_pairs(w_lru_a[l], w_lru_x[l]).astype(BF16)

        q, k, v, y_lru = _mixer_in(h, cos_t, sin_t, batch, row(g_mix_pre[l]), win, conv_w[l], row(conv_b[l]),
                                   wgate, row(b_lru_a[l]), row(b_lru_x[l]), row(lru_lambda[l]),
                                   row(q_a_norm[l]), wqa, row(kv_a_norm[l]), wk, wv)
        late_w = [w_ffn2_gate, w_ffn2_up, w_ffn2_down, w_out]
        if all(_cast_rows(w.shape[1], (batch * seq) // v.shape[2]) for w in late_w):
            y_mla, late_w = _attention(q, k, v, batch, to_bf16=[(w, l) for w in late_w])
        else:
            y_mla, _ = _attention(q, k, v, batch)
            late_w = [w[l].astype(BF16) for w in late_w]
        wg2, wu2, wd2, wo = late_w
        h = _ffn(h, row(g_ffn2_pre[l]), wg2, wu2, wd2, row(g_ffn2_post[l]),
                 mix=(y_lru, y_mla, wo, row(g_mix_post[l])))
    return h.reshape(batch, seq, d)
```

```python
import functools
import math

import jax
import jax.numpy as jnp
from jax import lax
from jax.experimental import pallas as pl
from jax.experimental.pallas import tpu as pltpu

F32 = jnp.float32
BF16 = jnp.bfloat16

EPS = 1e-6
CHUNK = 64
LRU_C = 8.0
ROPE_THETA = 10000.0
MLA_HEADS = 8
QK_NOPE_DIM = 64
QK_ROPE_DIM = 32
V_HEAD_DIM = 64
HEAD_LANES = 128
MASK_VALUE = -1e30

VMEM_LIMIT_BYTES = 56 * 1024 * 1024
TOKEN_TILE = 512
FFN_TOKEN_TILE = 512
FFN_ROW_GROUPS = 2
MIXER_ROW_GROUPS = 1
SUBLANES = 8
LANES = 128
ONES_ROWS = 16
SCORES_AHEAD = 2
LOG2_E = 1.4426950408889634


def _rms(x, g):
    return x * lax.rsqrt(jnp.mean(x * x, axis=-1, keepdims=True) + EPS) * g


def _params(grid_rank):
    return pltpu.CompilerParams(dimension_semantics=("arbitrary",) * grid_rank, vmem_limit_bytes=VMEM_LIMIT_BYTES)


def _const_spec(shape):
    nd = len(shape)
    return pl.BlockSpec(shape, lambda *_: (0,) * nd, pipeline_mode=pl.Buffered(1))


def _rope_table(pos_ref, invf_ref, cos_ref, sin_ref):
    ang = invf_ref[...] * pos_ref[...].astype(F32)
    cos_ref[...] = jnp.cos(ang)
    sin_ref[...] = jnp.sin(ang)


def _ffn_kernel(*refs, with_mix_out, with_rope, ff_chunks):
    rope_refs = None
    if with_rope:
        rope_refs = (refs[-5], refs[-4], refs[-2], refs[-1])
        refs = refs[:-5] + (refs[-3],)
    if with_mix_out:
        (h_ref, ylru_ref, ymla_ref, wout_ref, gmix_ref,
         gpre_ref, wg_ref, wu_ref, wd_ref, gpost_ref, o_ref) = refs
    else:
        h_ref, gpre_ref, wg_ref, wu_ref, wd_ref, gpost_ref, o_ref = refs
    tm = h_ref.shape[0]
    groups = [slice(r0, r0 + tm // FFN_ROW_GROUPS) for r0 in range(0, tm, tm // FFN_ROW_GROUPS)]
    hs = []
    for rows in groups:
        h = h_ref[rows, :]
        if with_mix_out:
            y = jnp.concatenate([ylru_ref[rows, :], ymla_ref[rows, :]], axis=1)
            m = jnp.dot(y, wout_ref[...], preferred_element_type=F32)
            h = h + _rms(m, gmix_ref[...])
        hs.append(h)
    for idx, (rows, h) in enumerate(zip(groups, hs)):
        if idx == 1 and rope_refs is not None:
            _rope_table(*rope_refs)
        n = _rms(h, gpre_ref[...]).astype(BF16)
        f = None
        for lo, hi in ff_chunks:
            g = jnp.dot(n, wg_ref[:, lo:hi], preferred_element_type=F32)
            u = jnp.dot(n, wu_ref[:, lo:hi], preferred_element_type=F32)
            a = (g * jax.nn.sigmoid(g) * u).astype(BF16)
            part = jnp.dot(a, wd_ref[lo:hi, :], preferred_element_type=F32)
            f = part if f is None else f + part
        o_ref[rows, :] = h + 0.5 * _rms(f, gpost_ref[...])


def _ffn_chunks(d_ff, target=1536):
    chunks, lo = [], 0
    while lo < d_ff:
        hi = min(lo + target, d_ff)
        chunks.append((lo, hi))
        lo = hi
    return tuple(chunks)


def _ffn(h, gpre, wg, wu, wd, gpost, mix=None, positions=None):
    t, d = h.shape
    d_ff = wg.shape[1]
    tm = math.gcd(FFN_TOKEN_TILE, t)
    tok = lambda w: pl.BlockSpec((tm, w), lambda i: (i, 0))
    ins, specs = [h], [tok(d)]
    if mix is not None:
        ylru, ymla, wout, gmix = mix
        ins += [ylru, ymla, wout, gmix]
        specs += [tok(ylru.shape[1]), tok(ymla.shape[1]), _const_spec(wout.shape), _const_spec(gmix.shape)]
    ins += [gpre, wg, wu, wd, gpost]
    specs += [_const_spec(a.shape) for a in (gpre, wg, wu, wd, gpost)]
    out_specs, out_shape = [tok(d)], [jax.ShapeDtypeStruct((t, d), F32)]
    if positions is not None:
        half = QK_ROPE_DIM // 2
        inv_freq = 1.0 / (ROPE_THETA ** (jnp.arange(0, QK_ROPE_DIM, 2, dtype=F32) / QK_ROPE_DIM))
        ins += [positions.reshape(1, t), inv_freq.reshape(half, 1)]
        specs += [pl.BlockSpec((1, tm), lambda i: (0, i)), pl.BlockSpec((half, 1), lambda i: (0, 0))]
        out_specs += [pl.BlockSpec((half, tm), lambda i: (0, i))] * 2
        out_shape += [jax.ShapeDtypeStruct((half, t), F32)] * 2
    outs = pl.pallas_call(
        functools.partial(_ffn_kernel, with_mix_out=mix is not None, with_rope=positions is not None,
                          ff_chunks=_ffn_chunks(d_ff)),
        grid=(t // tm,),
        in_specs=specs,
        out_specs=out_specs,
        out_shape=out_shape,
        compiler_params=_params(1),
        name="ffn_mix_out" if mix is not None else "ffn",
    )(*ins)
    return outs[0] if positions is None else outs


def _mixer_in_kernel(h_ref, cos_ref, sin_ref, *refs, q_scale):
    consts, (q_out, k_out, v_out, ylru_out, xbuf, hbuf, hcar) = refs[:-7], refs[-7:]

    @pl.when(pl.program_id(1) == 0)
    def _():
        xbuf[:, 0:SUBLANES, :] = jnp.zeros((xbuf.shape[0], SUBLANES, LANES), F32)
        hcar[...] = jnp.zeros_like(hcar)

    tm = h_ref.shape[0]
    gm = tm // MIXER_ROW_GROUPS
    for r0 in range(0, tm, gm):
        rows = slice(r0, r0 + gm)
        _mixer_group(h_ref.at[rows, :], cos_ref.at[:, rows], sin_ref.at[:, rows], *consts,
                     q_out.at[:, rows], k_out.at[rows, :], v_out.at[0, :, rows], ylru_out.at[rows, :],
                     xbuf, hbuf, hcar, q_scale=q_scale)


def _mixer_group(h_ref, cos_ref, sin_ref, gpre_ref, win_ref, convw_ref, convb_ref, wgate_ref,
                 ba_ref, bx_ref, lam_ref, qg_ref, wqa_ref, kvg_ref, wk_ref, wv_ref,
                 q_out, k_out, v_out, ylru_out, xbuf, hbuf, hcar, *, q_scale):
    tm = h_ref.shape[0]
    lru_w = ylru_out.shape[1]
    q_rank = qg_ref.shape[1]
    kv_rank = kvg_ref.shape[1]
    heads = q_out.shape[0] // HEAD_LANES
    nt_dims = (((1,), (1,)), ((), ()))

    n = _rms(h_ref[...], gpre_ref[...]).astype(BF16)
    proj = jnp.dot(n, win_ref[...], preferred_element_type=F32)
    c0 = 2 * lru_w
    c1 = c0 + q_rank
    c2 = c1 + kv_rank
    xl = proj[:, :lru_w]
    gate = proj[:, lru_w:c0]
    kpe = proj[:, c2:c2 + HEAD_LANES]
    cos = cos_ref[...]
    sin = sin_ref[...]
    half_r = cos.shape[0]
    r_lo, r_mid, r_hi = QK_NOPE_DIM, QK_NOPE_DIM + half_r, QK_NOPE_DIM + 2 * half_r

    def rope_rows(slab):
        x1, x2 = slab[r_lo:r_mid], slab[r_mid:r_hi]
        return [slab[:r_lo], x1 * cos - x2 * sin, x2 * cos + x1 * sin, slab[r_hi:]]

    slabs = lru_w // LANES
    seg_len = tm // SUBLANES
    pitch = seg_len + SUBLANES
    cw = convw_ref[...]
    cb = convb_ref[...]
    taps = cw.shape[0]
    xc_slabs = []
    for c in range(slabs):
        lanes = slice(c * LANES, (c + 1) * LANES)
        for s in range(SUBLANES):
            lo = SUBLANES + s * pitch
            xbuf[c, lo:lo + seg_len, :] = xl[s * seg_len:(s + 1) * seg_len, lanes]
            if s + 1 < SUBLANES:
                xbuf[c, lo + seg_len:lo + pitch, :] = xl[(s + 1) * seg_len - SUBLANES:(s + 1) * seg_len, lanes]
        x = [xbuf[c, pl.ds(SUBLANES + t, SUBLANES, stride=pitch), :] for t in range(1 - taps, seg_len)]
        xbuf[c, 0:SUBLANES, :] = xl[tm - SUBLANES:, lanes]
        w = [cw[k:k + 1, lanes] for k in range(taps)]
        steps = []
        for t in range(seg_len):
            acc = cb[:, lanes]
            for k in range(taps):
                acc = acc + x[t + k] * w[k]
            steps.append(acc)
        xc_slabs.append(jnp.concatenate(steps, axis=0))
    xc = jnp.concatenate(xc_slabs, axis=1)

    xcb = xc.astype(BF16)
    half = lru_w // 2
    g0 = jnp.dot(xcb[:, :half], wgate_ref[0], preferred_element_type=F32)
    g1 = jnp.dot(xcb[:, half:], wgate_ref[1], preferred_element_type=F32)

    qn = _rms(proj[:, c0:c1], qg_ref[...]).astype(BF16)
    qa = lax.dot_general(wqa_ref[...], qn, nt_dims, preferred_element_type=F32)
    q_rows = []
    for hd in range(heads):
        q_rows += rope_rows(qa[hd * HEAD_LANES:(hd + 1) * HEAD_LANES])
    q_out[...] = (jnp.concatenate(q_rows, axis=0) * q_scale).astype(BF16)

    kvn = _rms(proj[:, c1:c2], kvg_ref[...]).astype(BF16)
    k_rope = jnp.concatenate(rope_rows(kpe.T), axis=0).T
    k_nope = jnp.dot(kvn, wk_ref[...], preferred_element_type=F32)
    k_out[...] = (k_nope + jnp.tile(k_rope, (1, heads))).astype(BF16)
    v_out[...] = lax.dot_general(wv_ref[...], kvn, nt_dims, preferred_element_type=F32).astype(BF16)

    sigmoid = lambda v: 0.5 * jnp.tanh(0.5 * v) + 0.5
    r = sigmoid(jnp.concatenate([g0[:, :half], g1[:, :half]], axis=1) + ba_ref[...])
    i = sigmoid(jnp.concatenate([g0[:, half:], g1[:, half:]], axis=1) + bx_ref[...])
    log_a = (-LRU_C * r) * jax.nn.softplus(-lam_ref[...])
    a = jnp.exp(log_a)
    th = jnp.tanh(log_a)
    sq = -2.0 * th / (1.0 - th)
    u = jnp.where(sq > 0.0, sq * lax.rsqrt(sq), 0.0) * (i * xc)

    step = lambda t, v: v[t * SUBLANES:(t + 1) * SUBLANES]
    end, decay = step(0, u), step(0, a)
    for t in range(1, seg_len):
        end = step(t, a) * end + step(t, u)
        decay = step(t, a) * decay
    seg = lax.broadcasted_iota(jnp.int32, end.shape, 0)
    shift = 1
    while shift < SUBLANES:
        keep = seg >= shift
        end = jnp.where(keep, decay * pltpu.roll(end, shift, 0) + end, end)
        decay = jnp.where(keep, decay * pltpu.roll(decay, shift, 0), decay)
        shift *= 2
    h_in = hcar[...]
    state = jnp.where(seg == 0, h_in, pltpu.roll(end + decay * h_in, 1, 0))
    for t in range(seg_len):
        state = step(t, a) * state + step(t, u)
        for c in range(slabs):
            hbuf[c, pl.ds(t, SUBLANES, stride=pitch), :] = state[:, c * LANES:(c + 1) * LANES]
    hcar[...] = jnp.broadcast_to(state[SUBLANES - 1:SUBLANES, :], state.shape)
    h_lru = jnp.concatenate(
        [jnp.concatenate([hbuf[c, s * pitch:s * pitch + seg_len, :] for s in range(SUBLANES)], axis=0)
         for c in range(slabs)], axis=1)
    ylru_out[...] = (h_lru * jax.nn.gelu(gate)).astype(BF16)


def _mixer_in(h, cos_t, sin_t, batch, gpre, win, convw, convb, wgate, ba, bx, lam, qg, wqa, kvg, wk, wv):
    t, d = h.shape
    seq = t // batch
    tm = min(TOKEN_TILE, seq)
    gm = tm // MIXER_ROW_GROUPS
    ns = seq // tm
    lru_w = convw.shape[1]
    tok = lambda w: pl.BlockSpec((tm, w), lambda b, s: (b * ns + s, 0))
    cols = lambda r: pl.BlockSpec((r, tm), lambda b, s: (0, b * ns + s))
    consts = (gpre, win, convw, convb, wgate, ba, bx, lam, qg, wqa, kvg, wk, wv)
    q_scale = float((QK_NOPE_DIM + QK_ROPE_DIM) ** -0.5 * LOG2_E)
    return pl.pallas_call(
        functools.partial(_mixer_in_kernel, q_scale=q_scale),
        grid=(batch, ns),
        in_specs=[tok(d), cols(cos_t.shape[0]), cols(sin_t.shape[0])] + [_const_spec(a.shape) for a in consts],
        out_specs=[cols(wqa.shape[0]), tok(wk.shape[1]),
                   pl.BlockSpec((1, wv.shape[0], tm), lambda b, s: (b * ns + s, 0, 0)), tok(lru_w)],
        out_shape=[jax.ShapeDtypeStruct((wqa.shape[0], t), BF16),
                   jax.ShapeDtypeStruct((t, wk.shape[1]), BF16),
                   jax.ShapeDtypeStruct((t // tm, wv.shape[0], tm), BF16),
                   jax.ShapeDtypeStruct((t, lru_w), BF16)],
        scratch_shapes=[pltpu.VMEM((lru_w // LANES, gm + SUBLANES * SUBLANES + SUBLANES, LANES), F32),
                        pltpu.VMEM((lru_w // LANES, gm + SUBLANES * SUBLANES, LANES), F32),
                        pltpu.VMEM((SUBLANES, lru_w), F32)],
        compiler_params=_params(2),
        name="mixer_in",
    )(h, cos_t, sin_t, *consts)


def _attn_kernel(q_ref, k_ref, vt_ref, *refs, n_cast):
    cast_in, o_ref, cast_out = refs[:n_cast], refs[n_cast], refs[n_cast + 1:2 * n_cast + 1]
    m_ref, acc_ref, s_ref = refs[2 * n_cast + 1:]
    for src, dst in zip(cast_in, cast_out):
        dst[...] = src[...].astype(dst.dtype)

    tile = q_ref.shape[1]
    half = tile // 2
    heads = q_ref.shape[0] // HEAD_LANES
    qi = pl.program_id(1)

    m_ref[...] = jnp.full(m_ref.shape, MASK_VALUE, F32)
    acc_ref[...] = jnp.zeros_like(acc_ref)
    ones_rows = jnp.ones((ONES_ROWS, tile), BF16)
    slab = lambda h: slice(h * HEAD_LANES, (h + 1) * HEAD_LANES)

    def v_rows(kj, h):
        return jnp.concatenate([vt_ref[kj, h * V_HEAD_DIM:(h + 1) * V_HEAD_DIM, :], ones_rows], axis=0)

    def rescale(h, m_cur):
        m_prev = m_ref[h]
        m_new = jnp.maximum(m_prev, m_cur)
        m_ref[h] = m_new
        return m_new, jnp.exp2(m_prev - m_new)

    def full_scores(kj, h, slot):
        kh = k_ref[pl.ds(pl.multiple_of(kj * tile, tile), tile), slab(h)]
        s_ref[slot] = jnp.dot(kh, q_ref[slab(h), :], preferred_element_type=F32)
        return jnp.max(s_ref[slot], axis=0, keepdims=True)

    def full_accumulate(kj, h, slot, m_cur):
        m_new, alpha = rescale(h, m_cur)
        pt = jnp.exp2(s_ref[slot] - m_new).astype(BF16)
        acc_ref[h] = acc_ref[h] * alpha + jnp.dot(v_rows(kj, h), pt, preferred_element_type=F32)

    def diag_scores(kj, h, slot):
        key_chunk = lax.broadcasted_iota(jnp.int32, (half, half), 0) // CHUNK
        query_chunk = lax.broadcasted_iota(jnp.int32, (half, half), 1) // CHUNK
        visible = key_chunk <= query_chunk
        row0 = pl.multiple_of(kj * tile, tile)
        qh = q_ref[slab(h), :]
        early = jnp.dot(k_ref[pl.ds(row0, half), slab(h)], qh, preferred_element_type=F32)
        late = jnp.dot(k_ref[pl.ds(row0 + half, half), slab(h)], qh[:, half:], preferred_element_type=F32)
        s_ref[slot, :half, :half] = jnp.where(visible, early[:, :half], MASK_VALUE)
        s_ref[slot, :half, half:] = early[:, half:]
        s_ref[slot, half:, half:] = jnp.where(visible, late, MASK_VALUE)
        m_left = jnp.max(s_ref[slot, :half, :half], axis=0, keepdims=True)
        m_right = jnp.maximum(jnp.max(s_ref[slot, :half, half:], axis=0, keepdims=True),
                              jnp.max(s_ref[slot, half:, half:], axis=0, keepdims=True))
        return jnp.concatenate([m_left, m_right], axis=1)

    def diag_accumulate(kj, h, slot, m_cur):
        m_new, alpha = rescale(h, m_cur)
        p_early = jnp.exp2(s_ref[slot, :half, :] - m_new).astype(BF16)
        p_late = jnp.exp2(s_ref[slot, half:, half:] - m_new[:, half:]).astype(BF16)
        vt = v_rows(kj, h)
        pv = jnp.dot(vt[:, :half], p_early, preferred_element_type=F32)
        pv_late = jnp.dot(vt[:, half:], p_late, preferred_element_type=F32)
        acc = acc_ref[h] * alpha + pv
        acc_ref[h] = jnp.concatenate([acc[:, :half], acc[:, half:] + pv_late], axis=1)

    def run(tiles):
        units = [(kj, h, diag) for kj, diag in tiles for h in range(heads)]

        def issue(n):
            kj, h, diag = units[n]
            return (diag_scores if diag else full_scores)(kj, h, n % heads)

        pending = [issue(n) for n in range(min(SCORES_AHEAD, len(units)))]
        for n, (kj, h, diag) in enumerate(units):
            if n + SCORES_AHEAD < len(units):
                pending.append(issue(n + SCORES_AHEAD))
            (diag_accumulate if diag else full_accumulate)(kj, h, n % heads, pending.pop(0))

    def tile_quad(j, carry):
        run([(4 * j + t, False) for t in range(4)])
        return carry

    lax.fori_loop(0, qi // 4, tile_quad, 0)

    @pl.when(qi % 4 >= 2)
    def _():
        first = (qi // 4) * 4
        run([(first, False), (first + 1, False)])

    @pl.when(qi % 2 == 1)
    def _():
        run([(qi - 1, False)])

    run([(qi, True)])

    outs = []
    for h in range(heads):
        a = acc_ref[h]
        outs.append(a[:V_HEAD_DIM] / a[V_HEAD_DIM:V_HEAD_DIM + 1])
    o_ref[...] = jnp.concatenate(outs, axis=0).T.astype(o_ref.dtype)


def _attention(qt, k, vt, batch, to_bf16=()):
    t = k.shape[0]
    seq = t // batch
    tile = vt.shape[2]
    assert tile % (2 * CHUNK) == 0 and seq % tile == 0
    nt = seq // tile
    heads = k.shape[1] // HEAD_LANES
    d_v = vt.shape[1]
    steps = batch * nt
    cast_in_specs, cast_out_specs = [], []
    for w, layer in to_bf16:
        rows = _cast_rows(w.shape[1], steps)
        last = w.shape[1] // rows - 1
        cast_in_specs.append(pl.BlockSpec(
            (None, rows, w.shape[2]), lambda b, i, layer=layer, last=last: (layer, jnp.minimum(b * nt + i, last), 0)))
        cast_out_specs.append(pl.BlockSpec(
            (rows, w.shape[2]), lambda b, i, last=last: (jnp.minimum(b * nt + i, last), 0)))
    outs = pl.pallas_call(
        functools.partial(_attn_kernel, n_cast=len(to_bf16)),
        grid=(batch, nt),
        in_specs=[pl.BlockSpec((qt.shape[0], tile), lambda b, i: (0, b * nt + i)),
                  pl.BlockSpec((seq, k.shape[1]), lambda b, i: (b, 0)),
                  pl.BlockSpec((nt, d_v, tile), lambda b, i: (b, 0, 0))] + cast_in_specs,
        out_specs=[pl.BlockSpec((tile, d_v), lambda b, i: (b * nt + i, 0))] + cast_out_specs,
        scratch_shapes=[pltpu.VMEM((heads, 1, tile), F32),
                        pltpu.VMEM((heads, V_HEAD_DIM + ONES_ROWS, tile), F32),
                        pltpu.VMEM((heads, tile, tile), F32)],
        out_shape=[jax.ShapeDtypeStruct((t, d_v), BF16)]
                  + [jax.ShapeDtypeStruct(w.shape[1:], BF16) for w, _ in to_bf16],
        compiler_params=_params(2),
        name="attention",
    )(qt, k, vt, *[w for w, _ in to_bf16])
    return outs[0], outs[1:]


def _cast_rows(n_rows, steps):
    for rows in range(2 * SUBLANES, n_rows + 1, 2 * SUBLANES):
        if n_rows % rows == 0 and n_rows // rows <= steps:
            return rows
    return None


def _head_slab(nope, rope):
    rows, heads = rope.shape[0], rope.shape[1]
    if nope is None:
        nope = jnp.zeros((rows, heads, QK_NOPE_DIM), rope.dtype)
    pad = jnp.zeros((rows, heads, HEAD_LANES - QK_NOPE_DIM - QK_ROPE_DIM), rope.dtype)
    return jnp.concatenate([nope, rope, pad], axis=-1).reshape(rows, heads * HEAD_LANES)


def _block_diag_pairs(w_a, w_x):
    blocks, bd, _ = w_a.shape
    eye = jnp.eye(blocks, dtype=w_a.dtype)
    width = blocks * bd
    half = width // 2
    dense = lambda w: jnp.einsum('nde,nm->ndme', w, eye).reshape(width, width)
    da, dx = dense(w_a), dense(w_x)
    tiles = [jnp.concatenate([da[lo:lo + half, lo:lo + half], dx[lo:lo + half, lo:lo + half]], axis=1)
             for lo in (0, half)]
    return jnp.stack(tiles)


def kernel(x, positions, g_ffn1_pre, g_ffn1_post, w_ffn1_gate, w_ffn1_up, w_ffn1_down, g_mix_pre, g_mix_post, w_in, conv_w, conv_b, w_lru_a, b_lru_a, w_lru_x, b_lru_x, lru_lambda, q_a_norm, w_q_b, kv_a_norm, w_kv_b, w_out, g_ffn2_pre, g_ffn2_post, w_ffn2_gate, w_ffn2_up, w_ffn2_down):
    batch, seq, d = x.shape
    depth = g_ffn1_pre.shape[0]
    lru_w = conv_w.shape[-1]
    q_rank = q_a_norm.shape[-1]
    kv_rank = kv_a_norm.shape[-1]
    row = lambda v: v.reshape(1, -1)

    h = x.reshape(batch * seq, d)
    for l in range(depth):
        ffn1 = _ffn(h, row(g_ffn1_pre[l]), w_ffn1_gate[l].astype(BF16), w_ffn1_up[l].astype(BF16),
                    w_ffn1_down[l].astype(BF16), row(g_ffn1_post[l]),
                    positions=positions.reshape(-1) if l == 0 else None)
        if l == 0:
            h, cos_t, sin_t = ffn1
        else:
            h = ffn1

        wi = w_in[l]
        c_rope = 2 * lru_w + q_rank + kv_rank
        k_rope_w = wi[:, c_rope:].reshape(d, 1, QK_ROPE_DIM)
        win = jnp.concatenate([wi[:, :c_rope], _head_slab(None, k_rope_w)], axis=1).astype(BF16)
        wq = w_q_b[l].reshape(q_rank, MLA_HEADS, QK_NOPE_DIM + QK_ROPE_DIM)
        wqa = _head_slab(wq[..., :QK_NOPE_DIM], wq[..., QK_NOPE_DIM:]).T.astype(BF16)
        wkv = w_kv_b[l].reshape(kv_rank, MLA_HEADS, QK_NOPE_DIM + V_HEAD_DIM)
        wk = _head_slab(wkv[..., :QK_NOPE_DIM], jnp.zeros((kv_rank, MLA_HEADS, QK_ROPE_DIM), F32)).astype(BF16)
        wv = wkv[..., QK_NOPE_DIM:].reshape(kv_rank, MLA_HEADS * V_HEAD_DIM).T.astype(BF16)
        wgate = _block_diag_pairs(w_lru_a[l], w_lru_x[l]).astype(BF16)

        q, k, v, y_lru = _mixer_in(h, cos_t, sin_t, batch, row(g_mix_pre[l]), win, conv_w[l], row(conv_b[l]),
                                   wgate, row(b_lru_a[l]), row(b_lru_x[l]), row(lru_lambda[l]),
                                   row(q_a_norm[l]), wqa, row(kv_a_norm[l]), wk, wv)
        late_w = [w_ffn2_gate, w_ffn2_up, w_ffn2_down, w_out]
        if all(_cast_rows(w.shape[1], (batch * seq) // v.shape[2]) for w in late_w):
            y_mla, late_w = _attention(q, k, v, batch, to_bf16=[(w, l) for w in late_w])
        else:
            y_mla, _ = _attention(q, k, v, batch)
            late_w = [w[l].astype(BF16) for w in late_w]
        wg2, wu2, wd2, wo = late_w
        h = _ffn(h, row(g_ffn2_pre[l]), wg2, wu2, wd2, row(g_ffn2_post[l]),
                 mix=(y_lru, y_mla, wo, row(g_mix_post[l])))
    return h.reshape(batch, seq, d)
```
